```python
import jax, jax.numpy as jnp
from jax import lax
import numpy as np

D_MODEL = 1024
BATCH = 16
SEQ = 2048
DEPTH = 2
DEC_BATCH = 128
DEC_SEQ = 8
PAST_LEN = 16384
PAGE_SIZE = 128

HEAD_DIM = 64
N_HEADS = D_MODEL // 128
KV_HEADS = N_HEADS // 4
Q_W = N_HEADS * HEAD_DIM
KV_W = KV_HEADS * HEAD_DIM
WINDOW = 128
CHUNK = 128
GM_GROUPS = 4
GM_W = D_MODEL // 2
GM_GW = GM_W // GM_GROUPS
SC_W = D_MODEL // 2
CONV_W = 3
D_FF = 2816
N_BRANCH = 3
IN_COLS = Q_W + 2 * KV_W + 2 * GM_W + 3 * SC_W
ALPHA = (2.0 * DEPTH) ** 0.25
BETA = (8.0 * DEPTH) ** -0.25
LN_EPS = 1e-5

kernel_name = "hybrid_swa_gmlp_shortconv_convffn_deepnorm_step"


def layer_norm(x, g, b):
    xf = x.astype(jnp.float32)
    mu = xf.mean(-1, keepdims=True)
    var = jnp.square(xf - mu).mean(-1, keepdims=True)
    return ((xf - mu) * lax.rsqrt(var + LN_EPS) * g.astype(jnp.float32) + b.astype(jnp.float32)).astype(x.dtype)


def causal_dwconv(x, w, past):
    T = x.shape[1]
    xp = jnp.concatenate([past.astype(x.dtype), x], axis=1)
    y = w[0] * xp[:, 0:T]
    for k in range(1, CONV_W):
        y = y + w[k] * xp[:, k:k + T]
    return y, xp[:, -(CONV_W - 1):]


def alibi_slopes():
    return 2.0 ** (-8.0 * jnp.arange(1, N_HEADS + 1, dtype=jnp.float32) / N_HEADS)


def swa_attend(q, kk, vv, key_valid, sinks):
    B, N, T = q.shape[:3]
    S = kk.shape[2]
    G = N_HEADS // KV_HEADS
    qg = q.reshape(B, N, T, KV_HEADS, G, HEAD_DIM)
    s = jnp.einsum('bntkgd,bnskd->bnkgts', qg, kk).astype(jnp.float32) * (HEAD_DIM ** -0.5)
    dist_i = jnp.arange(T)[:, None] + WINDOW - jnp.arange(S)[None, :]
    slopes = alibi_slopes().reshape(KV_HEADS, G)
    s = s - slopes[:, :, None, None] * dist_i.astype(jnp.float32)
    allowed = (dist_i >= 0) & (dist_i <= WINDOW)
    mask = allowed[None, :, :] & key_valid[:, None, :]
    s = jnp.where(mask[None, :, None, None], s, -jnp.inf)
    sink = sinks.astype(jnp.float32).reshape(KV_HEADS, G)[:, :, None, None]
    m = jnp.maximum(s.max(-1, keepdims=True), sink)
    p = jnp.exp(s - m)
    denom = p.sum(-1, keepdims=True) + jnp.exp(sink - m)
    o = jnp.einsum('bnkgts,bnskd->bntkgd', (p / denom).astype(vv.dtype), vv)
    return o.reshape(B, N * T, Q_W)


def token_mixer(x, prompt, k_buf, v_buf, conv_past, w_in, w_gate, b_gate, gmlp_ln_g, gmlp_ln_b,
                gmlp_ws, gmlp_bs, mixconv_w, sinks, p_attn, p_gmlp, p_conv, w_o):
    B, T, _ = x.shape
    h = x @ w_in
    offs = np.cumsum([Q_W, KV_W, KV_W, GM_W, GM_W, SC_W, SC_W]).tolist()
    q, k, v, gu, gv, sb, sc, sh = jnp.split(h, offs, axis=-1)

    q = q.reshape(B, T, N_HEADS, HEAD_DIM)
    k = k.reshape(B, T, KV_HEADS, HEAD_DIM)
    v = v.reshape(B, T, KV_HEADS, HEAD_DIM)
    if prompt:
        N = T // WINDOW
        qb = q.reshape(B, N, WINDOW, N_HEADS, HEAD_DIM)
        kb = k.reshape(B, N, WINDOW, KV_HEADS, HEAD_DIM)
        vb = v.reshape(B, N, WINDOW, KV_HEADS, HEAD_DIM)
        kk = jnp.concatenate([jnp.concatenate([jnp.zeros_like(kb[:, :1]), kb[:, :-1]], axis=1), kb], axis=2)
        vv = jnp.concatenate([jnp.concatenate([jnp.zeros_like(vb[:, :1]), vb[:, :-1]], axis=1), vb], axis=2)
        key_valid = (jnp.arange(N)[:, None] * WINDOW + jnp.arange(2 * WINDOW)[None, :] - WINDOW) >= 0
        attn = swa_attend(qb, kk, vv, key_valid, sinks)
        new_k, new_v = k[:, -WINDOW:], v[:, -WINDOW:]
    else:
        kk = jnp.concatenate([k_buf.astype(k.dtype), k], axis=1)
        vv = jnp.concatenate([v_buf.astype(v.dtype), v], axis=1)
        key_valid = jnp.ones((1, WINDOW + T), dtype=bool)
        attn = swa_attend(q[:, None], kk[:, None], vv[:, None], key_valid, sinks)
        new_k, new_v = kk[:, -WINDOW:], vv[:, -WINDOW:]

    gv = layer_norm(gv, gmlp_ln_g, gmlp_ln_b)
    Tc = CHUNK if prompt else T
    N = T // Tc
    vr = gv.reshape(B, N, Tc, GM_GROUPS, GM_GW)
    ws = jnp.tril(gmlp_ws[:, :Tc, :Tc])
    sv = jnp.einsum('gts,bnsgc->bntgc', ws, vr) + gmlp_bs[:, :Tc].T[None, None, :, :, None]
    gm = gu * sv.reshape(B, T, GM_W)

    past = jnp.zeros((B, CONV_W - 1, SC_W), x.dtype) if prompt else conv_past
    cz, new_conv = causal_dwconv(sc * sh, mixconv_w, past)
    scv = sb * cz

    gates = jax.nn.sigmoid(x @ w_gate + b_gate).reshape(B, T, N_BRANCH, D_MODEL)
    merged = (gates[:, :, 0] * (attn @ p_attn) + gates[:, :, 1] * (gm @ p_gmlp)
              + gates[:, :, 2] * (scv @ p_conv))
    return merged @ w_o, new_k, new_v, new_conv, gv


def conv_ffn(x, prompt, past, w_up, conv_w, conv_b, w_down):
    B = x.shape[0]
    up = x @ w_up
    if prompt:
        past = jnp.zeros((B, CONV_W - 1, 2 * D_FF), x.dtype)
    c, new_past = causal_dwconv(up, conv_w, past)
    c = c + conv_b
    a, g = jnp.split(c, [D_FF], axis=-1)
    return (jax.nn.silu(g) * a) @ w_down, new_past


def setup_inputs(seed: int = 0) -> dict:
    key = jax.random.key(seed)
    ks = iter(jax.random.split(key, 32))

    def nrm(shape, scale):
        return jax.random.normal(next(ks), shape, jnp.float32) * scale

    L, D = DEPTH, D_MODEL
    return {
        "x_prompt": nrm((BATCH, SEQ, D), 1.0),
        "x_sample": nrm((DEC_BATCH, DEC_SEQ, D), 1.0),
        "cache_k_win": nrm((L, DEC_BATCH, WINDOW, KV_HEADS, HEAD_DIM), 1.0),
        "cache_v_win": nrm((L, DEC_BATCH, WINDOW, KV_HEADS, HEAD_DIM), 1.0),
        "state_mixconv": nrm((L, DEC_BATCH, CONV_W - 1, SC_W), 1.0),
        "state_ffnconv": nrm((L, DEC_BATCH, CONV_W - 1, 2 * D_FF), 1.0),
        "w_in": nrm((L, D, IN_COLS), D ** -0.5),
        "w_gate": nrm((L, D, N_BRANCH * D), D ** -0.5),
        "b_gate": nrm((L, N_BRANCH * D), 0.1),
        "gmlp_ln_g": 1.0 + nrm((L, GM_W), 0.1),
        "gmlp_ln_b": nrm((L, GM_W), 0.1),
        "gmlp_ws": nrm((L, GM_GROUPS, CHUNK, CHUNK), CHUNK ** -0.5),
        "gmlp_bs": 1.0 + nrm((L, GM_GROUPS, CHUNK), 0.1),
        "mixconv_w": nrm((L, CONV_W, SC_W), CONV_W ** -0.5),
        "attn_sinks": nrm((L, N_HEADS), 0.5),
        "p_attn": nrm((L, Q_W, D), BETA * Q_W ** -0.5),
        "p_gmlp": nrm((L, GM_W, D), BETA * GM_W ** -0.5),
        "p_conv": nrm((L, SC_W, D), BETA * SC_W ** -0.5),
        "w_o": nrm((L, D, D), BETA * D ** -0.5),
        "ln1_g": 1.0 + nrm((L, D), 0.1),
        "ln1_b": nrm((L, D), 0.1),
        "w_up": nrm((L, D, 2 * D_FF), D ** -0.5),
        "ffn_conv_w": nrm((L, CONV_W, 2 * D_FF), CONV_W ** -0.5),
        "ffn_conv_b": nrm((L, 2 * D_FF), 0.02),
        "w_down": nrm((L, D_FF, D), BETA * D_FF ** -0.5),
        "ln2_g": 1.0 + nrm((L, D), 0.1),
        "ln2_b": nrm((L, D), 0.1),
    }


def reference(x_prompt, x_sample, cache_k_win, cache_v_win, state_mixconv, state_ffnconv,
              w_in, w_gate, b_gate, gmlp_ln_g, gmlp_ln_b, gmlp_ws, gmlp_bs, mixconv_w, attn_sinks,
              p_attn, p_gmlp, p_conv, w_o, ln1_g, ln1_b, w_up, ffn_conv_w, ffn_conv_b, w_down,
              ln2_g, ln2_b):
    xp, xs = x_prompt, x_sample
    kp, vp, mcp, fcp = [], [], [], []
    ksm, vsm, mcs, fcs, gvs = [], [], [], [], []
    for l in range(DEPTH):
        mix_w = (w_in[l], w_gate[l], b_gate[l], gmlp_ln_g[l], gmlp_ln_b[l], gmlp_ws[l], gmlp_bs[l],
                 mixconv_w[l], attn_sinks[l], p_attn[l], p_gmlp[l], p_conv[l], w_o[l])
        m, nk, nv, nc, _ = token_mixer(xp, True, None, None, None, *mix_w)
        xp = layer_norm(ALPHA * xp + m, ln1_g[l], ln1_b[l])
        f, nf = conv_ffn(xp, True, None, w_up[l], ffn_conv_w[l], ffn_conv_b[l], w_down[l])
        xp = layer_norm(ALPHA * xp + f, ln2_g[l], ln2_b[l])
        kp.append(nk); vp.append(nv); mcp.append(nc); fcp.append(nf)
        m, nk, nv, nc, gv = token_mixer(xs, False, cache_k_win[l], cache_v_win[l], state_mixconv[l], *mix_w)
        xs = layer_norm(ALPHA * xs + m, ln1_g[l], ln1_b[l])
        f, nf = conv_ffn(xs, False, state_ffnconv[l], w_up[l], ffn_conv_w[l], ffn_conv_b[l], w_down[l])
        xs = layer_norm(ALPHA * xs + f, ln2_g[l], ln2_b[l])
        ksm.append(nk); vsm.append(nv); mcs.append(nc); fcs.append(nf); gvs.append(gv)
    return (xp, xs,
            jnp.stack(kp), jnp.stack(vp), jnp.stack(mcp), jnp.stack(fcp),
            jnp.stack(ksm), jnp.stack(vsm), jnp.stack(mcs), jnp.stack(fcs), jnp.stack(gvs))
```

```python
import functools

import jax
import jax.numpy as jnp
from jax import lax
from jax.experimental import pallas as pl
from jax.experimental.pallas import tpu as pltpu

F32 = jnp.float32
BF16 = jnp.bfloat16

HEAD_DIM = 64
WINDOW = 128
CHUNK = 128
GM_GROUPS = 4
CONV_W = 3
N_BRANCH = 3
LN_EPS = 1e-5
Q_GROUP = 4

SUBLANES = 8
LANES = 128
VMEM_LIMIT_BYTES = 60 * 1024 * 1024

PROMPT_TILE = 256
SAMPLE_SEQS = 32
FF_CHUNK = 256
SAMPLE_KEYS = 256


def _dot(a, b):
    return jnp.dot(a, b, preferred_element_type=F32)


def _layer_norm(x, g, b):
    mu = jnp.mean(x, axis=-1, keepdims=True)
    xc = x - mu
    var = jnp.mean(xc * xc, axis=-1, keepdims=True)
    return xc * lax.rsqrt(var + LN_EPS) * g + b


def _sigmoid(x):
    return 1.0 / (1.0 + jnp.exp(-x))


def _softmax_pv(s, sink, vv):
    m = jnp.maximum(jnp.max(s, axis=-1, keepdims=True), sink)
    p = jnp.exp(s - m)
    denom = jnp.sum(p, axis=-1, keepdims=True) + jnp.exp(sink - m)
    if s.ndim == 2:
        o = _dot(p.astype(BF16), vv)
    else:
        o = jnp.einsum("bqk,bkd->bqd", p.astype(BF16), vv, preferred_element_type=F32)
    return o / denom


def _shift_rows_2d(tail, cur):
    t = cur.shape[0]
    ext = jnp.concatenate([tail, cur], axis=0)
    return ext[SUBLANES - 1:SUBLANES - 1 + t], ext[SUBLANES - 2:SUBLANES - 2 + t]


def _shift_rows_3d(past, cur):
    t = lax.broadcasted_iota(jnp.int32, cur.shape, 1)
    p0 = jnp.broadcast_to(past[:, 0:1, :], cur.shape)
    p1 = jnp.broadcast_to(past[:, 1:2, :], cur.shape)
    r1 = pltpu.roll(cur, 1, axis=1)
    r2 = pltpu.roll(cur, 2, axis=1)
    s1 = jnp.where(t >= 1, r1, p1)
    s2 = jnp.where(t >= 2, r2, jnp.where(t == 1, p1, p0))
    return s1, s2


def _merge_and_norm(x, xb, branches, w_gate_ref, b_gate_ref, p_refs, w_o_ref, g_ref, b_ref, alpha):
    d = x.shape[-1]
    merged = None
    for i, (br, p_ref) in enumerate(zip(branches, p_refs)):
        gate = _sigmoid(_dot(xb, w_gate_ref[:, i * d:(i + 1) * d]) + b_gate_ref[:, i * d:(i + 1) * d])
        term = gate * _dot(br.astype(BF16), p_ref[...])
        merged = term if merged is None else merged + term
    mix = _dot(merged.astype(BF16), w_o_ref[...])
    return _layer_norm(alpha * x + mix, g_ref[...], b_ref[...])


def _mixer_prompt_kernel(sinks_ref, x_ref, w_in_ref, w_gate_ref, b_gate_ref, lng_ref, lnb_ref,
                         ws_ref, bst_ref, mcw_ref, p_attn_ref, p_gmlp_ref, p_conv_ref, w_o_ref,
                         ln1g_ref, ln1b_ref,
                         y_ref, kwin_ref, vwin_ref, mc_ref,
                         kprev, vprev, ztail, *, alpha, n_heads):
    n = pl.program_id(1)
    t = x_ref.shape[1]
    kv_w = kprev.shape[1]
    q_w = n_heads * HEAD_DIM
    gm_w = lng_ref.shape[1]
    sc_w = mcw_ref.shape[1]

    @pl.when(n == 0)
    def _():
        kprev[...] = jnp.zeros_like(kprev)
        vprev[...] = jnp.zeros_like(vprev)
        ztail[...] = jnp.zeros_like(ztail)

    x = x_ref[0]
    xb = x.astype(BF16)
    off = [0]

    def proj(width):
        lo = off[0]
        off[0] = lo + width
        return _dot(xb, w_in_ref[:, lo:lo + width])

    q = proj(q_w) * (HEAD_DIM ** -0.5)
    k = proj(kv_w)
    v = proj(kv_w)
    gu = proj(gm_w)
    gv = proj(gm_w)
    sb = proj(sc_w)
    sc = proj(sc_w)
    sh = proj(sc_w)

    kext = jnp.concatenate([kprev[...], k], axis=0)
    vext = jnp.concatenate([vprev[...], v], axis=0)
    kprev[...] = k[t - WINDOW:]
    vprev[...] = v[t - WINDOW:]
    kwin_ref[0] = k[t - WINDOW:]
    vwin_ref[0] = v[t - WINDOW:]

    ii = lax.broadcasted_iota(jnp.int32, (WINDOW, 2 * WINDOW), 0)
    jj = lax.broadcasted_iota(jnp.int32, (WINDOW, 2 * WINDOW), 1)
    dist = ii + WINDOW - jj
    allowed = (dist >= 0) & (dist <= WINDOW)
    distf = dist.astype(F32)
    blocks = []
    for i in range(t // WINDOW):
        valid = (allowed & ((jj >= WINDOW) | (n > 0))) if i == 0 else allowed
        heads = []
        for kh in range(n_heads // Q_GROUP):
            kk = kext[i * WINDOW:(i + 2) * WINDOW, kh * HEAD_DIM:(kh + 1) * HEAD_DIM].astype(BF16)
            vv = vext[i * WINDOW:(i + 2) * WINDOW, kh * HEAD_DIM:(kh + 1) * HEAD_DIM].astype(BF16)
            for g in range(Q_GROUP):
                h = kh * Q_GROUP + g
                slope = 2.0 ** (-8.0 * (h + 1) / n_heads)
                qh = q[i * WINDOW:(i + 1) * WINDOW, h * HEAD_DIM:(h + 1) * HEAD_DIM].astype(BF16)
                s = lax.dot_general(qh, kk, (((1,), (1,)), ((), ())), preferred_element_type=F32)
                s = jnp.where(valid, s - slope * distf, -jnp.inf)
                heads.append(_softmax_pv(s, sinks_ref[h], vv))
        blocks.append(jnp.concatenate(heads, axis=1))
    attn = jnp.concatenate(blocks, axis=0)

    gvn = _layer_norm(gv, lng_ref[...], lnb_ref[...]).astype(BF16)
    rr = lax.broadcasted_iota(jnp.int32, (CHUNK, CHUNK), 0)
    cc = lax.broadcasted_iota(jnp.int32, (CHUNK, CHUNK), 1)
    gw = gm_w // GM_GROUPS
    cols = []
    for g in range(GM_GROUPS):
        wsg = jnp.where(rr >= cc, ws_ref[g], 0.0).astype(BF16)
        bias = bst_ref[:, g:g + 1]
        rows = [_dot(wsg, gvn[c * CHUNK:(c + 1) * CHUNK, g * gw:(g + 1) * gw]) + bias
                for c in range(t // CHUNK)]
        cols.append(jnp.concatenate(rows, axis=0))
    gm = gu * jnp.concatenate(cols, axis=1)

    z = sc * sh
    z1, z2 = _shift_rows_2d(ztail[...], z)
    ztail[...] = z[t - SUBLANES:]
    mc_ref[0] = z[t - SUBLANES:]
    scv = sb * (mcw_ref[0:1, :] * z2 + mcw_ref[1:2, :] * z1 + mcw_ref[2:3, :] * z)

    y_ref[0] = _merge_and_norm(x, xb, (attn, gm, scv), w_gate_ref, b_gate_ref,
                               (p_attn_ref, p_gmlp_ref, p_conv_ref), w_o_ref, ln1g_ref, ln1b_ref, alpha)


def _mixer_sample_kernel(sinks_ref, x_ref, ck_ref, cv_ref, mcs_ref, w_in_ref, w_gate_ref, b_gate_ref,
                         lng_ref, lnb_ref, wsbd_ref, bs8_ref, mcw_ref, p_attn_ref, p_gmlp_ref,
                         p_conv_ref, w_o_ref, ln1g_ref, ln1b_ref,
                         y_ref, knew_ref, vnew_ref, mc_ref, gvn_ref, *, alpha, n_heads):
    bt, t, d = x_ref.shape
    kv_w = ck_ref.shape[2]
    q_w = n_heads * HEAD_DIM
    gm_w = lng_ref.shape[1]
    sc_w = mcw_ref.shape[1]
    rows = bt * t

    x = x_ref[...].reshape(rows, d)
    xb = x.astype(BF16)
    off = [0]

    def proj(width):
        lo = off[0]
        off[0] = lo + width
        return _dot(xb, w_in_ref[:, lo:lo + width])

    q = (proj(q_w) * (HEAD_DIM ** -0.5)).reshape(bt, t, q_w)
    k = proj(kv_w).reshape(bt, t, kv_w)
    v = proj(kv_w).reshape(bt, t, kv_w)
    gu = proj(gm_w)
    gv = proj(gm_w)
    sb = proj(sc_w)
    sc = proj(sc_w)
    sh = proj(sc_w)

    ck = ck_ref[...]
    cv = cv_ref[...]
    knew_ref[:, 0:WINDOW - t, :] = ck[:, t:, :]
    knew_ref[:, WINDOW - t:, :] = k
    vnew_ref[:, 0:WINDOW - t, :] = cv[:, t:, :]
    vnew_ref[:, WINDOW - t:, :] = v
    pad = jnp.zeros((bt, SAMPLE_KEYS - WINDOW - t, kv_w), F32)
    kkf = jnp.concatenate([ck, k, pad], axis=1)
    vvf = jnp.concatenate([cv, v, pad], axis=1)

    qrows = Q_GROUP * t
    ri = lax.broadcasted_iota(jnp.int32, (qrows, SAMPLE_KEYS), 0)
    ci = lax.broadcasted_iota(jnp.int32, (qrows, SAMPLE_KEYS), 1)
    dist = ri % t + WINDOW - ci
    valid = (dist >= 0) & (dist <= WINDOW) & (ci < WINDOW + t)
    distf = dist.astype(F32)
    gi = ri // t
    head_outs = []
    for kh in range(n_heads // Q_GROUP):
        kk = kkf[:, :, kh * HEAD_DIM:(kh + 1) * HEAD_DIM].astype(BF16)
        vv = vvf[:, :, kh * HEAD_DIM:(kh + 1) * HEAD_DIM].astype(BF16)
        qg = jnp.concatenate(
            [q[:, :, (kh * Q_GROUP + g) * HEAD_DIM:(kh * Q_GROUP + g + 1) * HEAD_DIM] for g in range(Q_GROUP)],
            axis=1).astype(BF16)
        s = jnp.einsum("bqd,bkd->bqk", qg, kk, preferred_element_type=F32)
        slope = jnp.zeros((qrows, SAMPLE_KEYS), F32)
        sink = jnp.zeros((qrows, 1), F32)
        for g in range(Q_GROUP):
            h = kh * Q_GROUP + g
            slope = jnp.where(gi == g, 2.0 ** (-8.0 * (h + 1) / n_heads), slope)
            sink = jnp.where(gi[:, 0:1] == g, sinks_ref[h], sink)
        bias = jnp.where(valid, -slope * distf, -jnp.inf)
        o = _softmax_pv(s + bias[None], sink[None], vv)
        head_outs += [o[:, g * t:(g + 1) * t, :] for g in range(Q_GROUP)]
    attn = jnp.concatenate(head_outs, axis=2).reshape(rows, q_w)

    gvn = _layer_norm(gv, lng_ref[...], lnb_ref[...])
    gvn_ref[...] = gvn.reshape(bt, t, gm_w)
    gvb = gvn.astype(BF16)
    gw = gm_w // GM_GROUPS
    sv = jnp.concatenate([_dot(wsbd_ref[g], gvb[:, g * gw:(g + 1) * gw]) for g in range(GM_GROUPS)], axis=1)
    sv = sv.reshape(bt, t, gm_w) + bs8_ref[...][None]
    gm = gu * sv.reshape(rows, gm_w)

    z = (sc * sh).reshape(bt, t, sc_w)
    z1, z2 = _shift_rows_3d(mcs_ref[...], z)
    mc_ref[...] = z
    cz = mcw_ref[0:1, :][None] * z2 + mcw_ref[1:2, :][None] * z1 + mcw_ref[2:3, :][None] * z
    scv = sb * cz.reshape(rows, sc_w)

    y = _merge_and_norm(x, xb, (attn, gm, scv), w_gate_ref, b_gate_ref,
                        (p_attn_ref, p_gmlp_ref, p_conv_ref), w_o_ref, ln1g_ref, ln1b_ref, alpha)
    y_ref[...] = y.reshape(bt, t, d)


def _ffn_core(x, up_ref, cw_ref, cb_ref, w_down_ref, g_ref, b_ref, hbuf, shift, store_up, alpha):
    d_ff = w_down_ref.shape[0]
    xb = x.astype(BF16)
    for j in range(d_ff // FF_CHUNK):
        parts = []
        for base in (0, d_ff):
            lo = base + j * FF_CHUNK
            up = _dot(xb, up_ref[:, lo:lo + FF_CHUNK])
            u1, u2 = shift(up, lo)
            store_up(up, lo)
            parts.append(cw_ref[0:1, lo:lo + FF_CHUNK] * u2 + cw_ref[1:2, lo:lo + FF_CHUNK] * u1
                         + cw_ref[2:3, lo:lo + FF_CHUNK] * up + cb_ref[:, lo:lo + FF_CHUNK])
        a, g = parts
        hbuf[:, j * FF_CHUNK:(j + 1) * FF_CHUNK] = (g * _sigmoid(g) * a).astype(BF16)
    f = _dot(hbuf[...], w_down_ref[...])
    return _layer_norm(alpha * x + f, g_ref[...], b_ref[...])


def _ffn_prompt_kernel(x_ref, up_ref, cw_ref, cb_ref, w_down_ref, g_ref, b_ref,
                       y_ref, fc_ref, tail, hbuf, *, alpha):
    t = x_ref.shape[1]

    @pl.when(pl.program_id(1) == 0)
    def _():
        tail[...] = jnp.zeros_like(tail)

    def shift(up, lo):
        return _shift_rows_2d(tail[:, lo:lo + FF_CHUNK], up)

    def store_up(up, lo):
        tail[:, lo:lo + FF_CHUNK] = up[t - SUBLANES:]
        fc_ref[0, :, lo:lo + FF_CHUNK] = up[t - SUBLANES:]

    y_ref[0] = _ffn_core(x_ref[0], up_ref, cw_ref, cb_ref, w_down_ref, g_ref, b_ref, hbuf,
                         shift, store_up, alpha)


def _ffn_sample_kernel(x_ref, past_ref, up_ref, cw_ref, cb_ref, w_down_ref, g_ref, b_ref,
                       y_ref, fc_ref, hbuf, *, alpha):
    bt, t, d = x_ref.shape

    def shift(up, lo):
        u1, u2 = _shift_rows_3d(past_ref[:, :, lo:lo + FF_CHUNK], up.reshape(bt, t, FF_CHUNK))
        return u1.reshape(bt * t, FF_CHUNK), u2.reshape(bt * t, FF_CHUNK)

    def store_up(up, lo):
        fc_ref[:, :, lo:lo + FF_CHUNK] = up.reshape(bt, t, FF_CHUNK)

    y = _ffn_core(x_ref[...].reshape(bt * t, d), up_ref, cw_ref, cb_ref, w_down_ref, g_ref, b_ref, hbuf,
                  shift, store_up, alpha)
    y_ref[...] = y.reshape(bt, t, d)


def _resident(shape):
    zeros = (0,) * len(shape)
    return pl.BlockSpec(shape, lambda *_: zeros, pipeline_mode=pl.Buffered(1))


def _params():
    return pltpu.CompilerParams(dimension_semantics=("arbitrary", "arbitrary"),
                                vmem_limit_bytes=VMEM_LIMIT_BYTES)


def _mixer_prompt(x, sinks, w, alpha, n_heads):
    b, s, d = x.shape
    t = PROMPT_TILE
    kv_w = (n_heads // Q_GROUP) * HEAD_DIM
    sc_w = w["mcw"].shape[1]
    weights = [w[k] for k in ("w_in", "w_gate", "b_gate", "lng", "lnb", "ws", "bst", "mcw",
                              "p_attn", "p_gmlp", "p_conv", "w_o", "ln1g", "ln1b")]
    tile = pl.BlockSpec((1, t, d), lambda i, j: (i, j, 0))
    per_seq = lambda rows, width: pl.BlockSpec((1, rows, width), lambda i, j: (i, 0, 0))
    return pl.pallas_call(
        functools.partial(_mixer_prompt_kernel, alpha=alpha, n_heads=n_heads),
        grid=(b, s // t),
        in_specs=[pl.BlockSpec(memory_space=pltpu.SMEM), tile] + [_resident(a.shape) for a in weights],
        out_specs=[tile, per_seq(WINDOW, kv_w), per_seq(WINDOW, kv_w), per_seq(SUBLANES, sc_w)],
        out_shape=[jax.ShapeDtypeStruct((b, s, d), F32),
                   jax.ShapeDtypeStruct((b, WINDOW, kv_w), F32),
                   jax.ShapeDtypeStruct((b, WINDOW, kv_w), F32),
                   jax.ShapeDtypeStruct((b, SUBLANES, sc_w), F32)],
        scratch_shapes=[pltpu.VMEM((WINDOW, kv_w), F32), pltpu.VMEM((WINDOW, kv_w), F32),
                        pltpu.VMEM((SUBLANES, sc_w), F32)],
        compiler_params=_params(),
        name="mixer_prompt",
    )(sinks, x, *weights)


def _mixer_sample(x, ck, cv, mcs, sinks, w, alpha, n_heads):
    b, t, d = x.shape
    bt = SAMPLE_SEQS
    kv_w = ck.shape[2]
    gm_w = w["lng"].shape[1]
    sc_w = w["mcw"].shape[1]
    weights = [w[k] for k in ("w_in", "w_gate", "b_gate", "lng", "lnb", "wsbd", "bs8", "mcw",
                              "p_attn", "p_gmlp", "p_conv", "w_o", "ln1g", "ln1b")]
    seqs = lambda rows, width: pl.BlockSpec((bt, rows, width), lambda i, j: (i, 0, 0))
    return pl.pallas_call(
        functools.partial(_mixer_sample_kernel, alpha=alpha, n_heads=n_heads),
        grid=(b // bt, 1),
        in_specs=[pl.BlockSpec(memory_space=pltpu.SMEM), seqs(t, d), seqs(WINDOW, kv_w), seqs(WINDOW, kv_w),
                  seqs(CONV_W - 1, sc_w)] + [_resident(a.shape) for a in weights],
        out_specs=[seqs(t, d), seqs(WINDOW, kv_w), seqs(WINDOW, kv_w), seqs(t, sc_w), seqs(t, gm_w)],
        out_shape=[jax.ShapeDtypeStruct((b, t, d), F32),
                   jax.ShapeDtypeStruct((b, WINDOW, kv_w), F32),
                   jax.ShapeDtypeStruct((b, WINDOW, kv_w), F32),
                   jax.ShapeDtypeStruct((b, t, sc_w), F32),
                   jax.ShapeDtypeStruct((b, t, gm_w), F32)],
        compiler_params=_params(),
        name="mixer_sample",
    )(sinks, x, ck, cv, mcs, *weights)


def _ffn_prompt(x, w, alpha):
    b, s, d = x.shape
    t = PROMPT_TILE
    d_ff = w["w_down"].shape[0]
    weights = [w[k] for k in ("w_up", "fcw", "fcb", "w_down", "ln2g", "ln2b")]
    tile = pl.BlockSpec((1, t, d), lambda i, j: (i, j, 0))
    return pl.pallas_call(
        functools.partial(_ffn_prompt_kernel, alpha=alpha),
        grid=(b, s // t),
        in_specs=[tile] + [_resident(a.shape) for a in weights],
        out_specs=[tile, pl.BlockSpec((1, SUBLANES, 2 * d_ff), lambda i, j: (i, 0, 0))],
        out_shape=[jax.ShapeDtypeStruct((b, s, d), F32),
                   jax.ShapeDtypeStruct((b, SUBLANES, 2 * d_ff), F32)],
        scratch_shapes=[pltpu.VMEM((SUBLANES, 2 * d_ff), F32), pltpu.VMEM((t, d_ff), BF16)],
        compiler_params=_params(),
        name="ffn_prompt",
    )(x, *weights)


def _ffn_sample(x, past, w, alpha):
    b, t, d = x.shape
    bt = SAMPLE_SEQS
    d_ff = w["w_down"].shape[0]
    weights = [w[k] for k in ("w_up", "fcw", "fcb", "w_down", "ln2g", "ln2b")]
    seqs = lambda rows, width: pl.BlockSpec((bt, rows, width), lambda i, j: (i, 0, 0))
    return pl.pallas_call(
        functools.partial(_ffn_sample_kernel, alpha=alpha),
        grid=(b // bt, 1),
        in_specs=[seqs(t, d), seqs(CONV_W - 1, 2 * d_ff)] + [_resident(a.shape) for a in weights],
        out_specs=[seqs(t, d), seqs(t, 2 * d_ff)],
        out_shape=[jax.ShapeDtypeStruct((b, t, d), F32),
                   jax.ShapeDtypeStruct((b, t, 2 * d_ff), F32)],
        scratch_shapes=[pltpu.VMEM((bt * t, d_ff), BF16)],
        compiler_params=_params(),
        name="ffn_sample",
    )(x, past, *weights)


def kernel(x_prompt, x_sample, cache_k_win, cache_v_win, state_mixconv, state_ffnconv, w_in, w_gate, b_gate, gmlp_ln_g, gmlp_ln_b, gmlp_ws, gmlp_bs, mixconv_w, attn_sinks, p_attn, p_gmlp, p_conv, w_o, ln1_g, ln1_b, w_up, ffn_conv_w, ffn_conv_b, w_down, ln2_g, ln2_b):
    depth = w_in.shape[0]
    d = x_prompt.shape[-1]
    n_heads = d // 128
    kv_heads = n_heads // Q_GROUP
    alpha = (2.0 * depth) ** 0.25
    dec_b, dec_t = x_sample.shape[:2]
    gm_w = gmlp_ln_g.shape[1]
    gw = gm_w // GM_GROUPS
    rows_s = SAMPLE_SEQS * dec_t

    ws_t = jnp.tril(gmlp_ws[:, :, :dec_t, :dec_t])
    eye = jnp.eye(SAMPLE_SEQS, dtype=F32)
    wsbd = jnp.einsum("ab,lgts->lgatbs", eye, ws_t).reshape(depth, GM_GROUPS, rows_s, rows_s).astype(BF16)
    bs8 = jnp.repeat(jnp.swapaxes(gmlp_bs[:, :, :dec_t], 1, 2), gw, axis=2)

    xp, xs = x_prompt, x_sample
    ck_all = cache_k_win.reshape(depth, dec_b, WINDOW, kv_heads * HEAD_DIM)
    cv_all = cache_v_win.reshape(depth, dec_b, WINDOW, kv_heads * HEAD_DIM)
    outs = {k: [] for k in ("kp", "vp", "mcp", "fcp", "ks", "vs", "mcs", "fcs", "gvs")}
    for l in range(depth):
        w = dict(
            w_in=w_in[l].astype(BF16), w_gate=w_gate[l].astype(BF16), b_gate=b_gate[l][None],
            lng=gmlp_ln_g[l][None], lnb=gmlp_ln_b[l][None], ws=gmlp_ws[l], bst=gmlp_bs[l].T,
            wsbd=wsbd[l], bs8=bs8[l], mcw=mixconv_w[l],
            p_attn=p_attn[l].astype(BF16), p_gmlp=p_gmlp[l].astype(BF16), p_conv=p_conv[l].astype(BF16),
            w_o=w_o[l].astype(BF16), ln1g=ln1_g[l][None], ln1b=ln1_b[l][None],
            w_up=w_up[l].astype(BF16), fcw=ffn_conv_w[l], fcb=ffn_conv_b[l][None],
            w_down=w_down[l].astype(BF16), ln2g=ln2_g[l][None], ln2b=ln2_b[l][None])
        sinks = attn_sinks[l]

        xp, kwin, vwin, mc = _mixer_prompt(xp, sinks, w, alpha, n_heads)
        xp, fc = _ffn_prompt(xp, w, alpha)
        outs["kp"].append(kwin.reshape(-1, WINDOW, kv_heads, HEAD_DIM))
        outs["vp"].append(vwin.reshape(-1, WINDOW, kv_heads, HEAD_DIM))
        outs["mcp"].append(mc[:, SUBLANES - (CONV_W - 1):])
        outs["fcp"].append(fc[:, SUBLANES - (CONV_W - 1):])

        xs, knew, vnew, mcn, gvn = _mixer_sample(xs, ck_all[l], cv_all[l], state_mixconv[l], sinks, w,
                                                 alpha, n_heads)
        xs, fcn = _ffn_sample(xs, state_ffnconv[l], w, alpha)
        outs["ks"].append(knew.reshape(-1, WINDOW, kv_heads, HEAD_DIM))
        outs["vs"].append(vnew.reshape(-1, WINDOW, kv_heads, HEAD_DIM))
        outs["mcs"].append(mcn[:, dec_t - (CONV_W - 1):])
        outs["fcs"].append(fcn[:, dec_t - (CONV_W - 1):])
        outs["gvs"].append(gvn)
    st = {k: jnp.stack(v) for k, v in outs.items()}
    return (xp, xs, st["kp"], st["vp"], st["mcp"], st["fcp"],
            st["ks"], st["vs"], st["mcs"], st["fcs"], st["gvs"])
```

```python
import functools

import jax
import jax.numpy as jnp
from jax import lax
from jax.experimental import pallas as pl
from jax.experimental.pallas import tpu as pltpu

F32 = jnp.float32
BF16 = jnp.bfloat16

HEAD_DIM = 64
WINDOW = 128
CHUNK = 128
GM_GROUPS = 4
CONV_W = 3
N_BRANCH = 3
LN_EPS = 1e-5
LOG2_E = 1.4426950408889634
Q_GROUP = 4

SUBLANES = 8
LANES = 128
VMEM_LIMIT_BYTES = 60 * 1024 * 1024

PROMPT_TILE = 512
SAMPLE_SEQS = 32
FF_CHUNK = 256
SAMPLE_KEYS = 256


def _dot(a, b):
    return jnp.dot(a, b, preferred_element_type=F32)


def _layer_norm(x, g, b):
    mu = jnp.mean(x, axis=-1, keepdims=True)
    xc = x - mu
    var = jnp.mean(xc * xc, axis=-1, keepdims=True)
    return xc * lax.rsqrt(var + LN_EPS) * g + b


def _sigmoid(x):
    return 1.0 / (1.0 + jnp.exp2(x * (-LOG2_E)))


def _softmax_pv(s, sink, vv):
    m = jnp.maximum(jnp.max(s, axis=-1, keepdims=True), sink)
    p = jnp.exp(s - m)
    denom = jnp.sum(p, axis=-1, keepdims=True) + jnp.exp(sink - m)
    if s.ndim == 2:
        o = _dot(p.astype(BF16), vv)
    else:
        o = jnp.einsum("bqk,bkd->bqd", p.astype(BF16), vv, preferred_element_type=F32)
    return o / denom


def _shift_rows_roll(tail, cur):
    row = lax.broadcasted_iota(jnp.int32, tail.shape, 0)
    outs = []
    for k in (1, 2):
        r = pltpu.roll(cur, k, axis=0)
        head = jnp.where(row < k, pltpu.roll(tail, k, axis=0), r[:SUBLANES])
        outs.append(jnp.concatenate([head, r[SUBLANES:]], axis=0))
    return outs


def _shift_rows_3d(past, cur):
    t = lax.broadcasted_iota(jnp.int32, cur.shape, 1)
    p0 = jnp.broadcast_to(past[:, 0:1, :], cur.shape)
    p1 = jnp.broadcast_to(past[:, 1:2, :], cur.shape)
    r1 = pltpu.roll(cur, 1, axis=1)
    r2 = pltpu.roll(cur, 2, axis=1)
    s1 = jnp.where(t >= 1, r1, p1)
    s2 = jnp.where(t >= 2, r2, jnp.where(t == 1, p1, p0))
    return s1, s2


def _merge_and_norm(x, xb, branches, w_gate_ref, b_gate_ref, p_refs, w_o_ref, g_ref, b_ref, alpha):
    d = x.shape[-1]
    merged = None
    for i, (br, p_ref) in enumerate(zip(branches, p_refs)):
        gate = _sigmoid(_dot(xb, w_gate_ref[:, i * d:(i + 1) * d]) + b_gate_ref[:, i * d:(i + 1) * d])
        term = gate * _dot(br.astype(BF16), p_ref[...])
        merged = term if merged is None else merged + term
    mix = _dot(merged.astype(BF16), w_o_ref[...])
    return _layer_norm(alpha * x + mix, g_ref[...], b_ref[...])


def _mixer_prompt_kernel(sinks_ref, x_ref, w_in_ref, w_gate_ref, b_gate_ref, lng_ref, lnb_ref,
                         ws_ref, bst_ref, mcw_ref, p_attn_ref, p_gmlp_ref, p_conv_ref, w_o_ref,
                         ln1g_ref, ln1b_ref,
                         y_ref, kwin_ref, vwin_ref, mc_ref,
                         kprev, vprev, ztail, *, alpha, n_heads, layer):
    n = pl.program_id(1)
    t = x_ref.shape[1]
    kv_w = kprev.shape[1]
    q_w = n_heads * HEAD_DIM
    gm_w = lng_ref.shape[1]
    sc_w = mcw_ref.shape[1]

    @pl.when(n == 0)
    def _():
        kprev[...] = jnp.zeros_like(kprev)
        vprev[...] = jnp.zeros_like(vprev)
        ztail[...] = jnp.zeros_like(ztail)

    x = x_ref[0]
    xb = x.astype(BF16)
    off = [0]

    def proj(width):
        lo = off[0]
        off[0] = lo + width
        return _dot(xb, w_in_ref[:, lo:lo + width])

    q = proj(q_w) * (HEAD_DIM ** -0.5)
    k = proj(kv_w)
    v = proj(kv_w)
    gu = proj(gm_w)
    gv = proj(gm_w)
    sb = proj(sc_w)
    sc = proj(sc_w)
    sh = proj(sc_w)

    kext = jnp.concatenate([kprev[...], k], axis=0)
    vext = jnp.concatenate([vprev[...], v], axis=0)
    kprev[...] = k[t - WINDOW:]
    vprev[...] = v[t - WINDOW:]
    kwin_ref[0] = k[t - WINDOW:]
    vwin_ref[0] = v[t - WINDOW:]

    ii = lax.broadcasted_iota(jnp.int32, (WINDOW, 2 * WINDOW), 0)
    jj = lax.broadcasted_iota(jnp.int32, (WINDOW, 2 * WINDOW), 1)
    dist = ii + WINDOW - jj
    allowed = (dist >= 0) & (dist <= WINDOW)
    distf = dist.astype(F32)
    blocks = []
    for i in range(t // WINDOW):
        valid = (allowed & ((jj >= WINDOW) | (n > 0))) if i == 0 else allowed
        heads = []
        for kh in range(n_heads // Q_GROUP):
            kk = kext[i * WINDOW:(i + 2) * WINDOW, kh * HEAD_DIM:(kh + 1) * HEAD_DIM].astype(BF16)
            vv = vext[i * WINDOW:(i + 2) * WINDOW, kh * HEAD_DIM:(kh + 1) * HEAD_DIM].astype(BF16)
            for g in range(Q_GROUP):
                h = kh * Q_GROUP + g
                slope = 2.0 ** (-8.0 * (h + 1) / n_heads)
                qh = q[i * WINDOW:(i + 1) * WINDOW, h * HEAD_DIM:(h + 1) * HEAD_DIM].astype(BF16)
                s = lax.dot_general(qh, kk, (((1,), (1,)), ((), ())), preferred_element_type=F32)
                s = jnp.where(valid, s - slope * distf, -jnp.inf)
                heads.append(_softmax_pv(s, sinks_ref[layer, h], vv))
        blocks.append(jnp.concatenate(heads, axis=1))
    attn = jnp.concatenate(blocks, axis=0)

    gvn = _layer_norm(gv, lng_ref[...], lnb_ref[...]).astype(BF16)
    rr = lax.broadcasted_iota(jnp.int32, (CHUNK, CHUNK), 0)
    cc = lax.broadcasted_iota(jnp.int32, (CHUNK, CHUNK), 1)
    gw = gm_w // GM_GROUPS
    cols = []
    for g in range(GM_GROUPS):
        wsg = jnp.where(rr >= cc, ws_ref[g], 0.0).astype(BF16)
        bias = bst_ref[:, g:g + 1]
        rows = [_dot(wsg, gvn[c * CHUNK:(c + 1) * CHUNK, g * gw:(g + 1) * gw]) + bias
                for c in range(t // CHUNK)]
        cols.append(jnp.concatenate(rows, axis=0))
    gm = gu * jnp.concatenate(cols, axis=1)

    z = sc * sh
    z1, z2 = _shift_rows_roll(ztail[...], z)
    ztail[...] = z[t - SUBLANES:]
    mc_ref[0] = z[t - SUBLANES:]
    scv = sb * (mcw_ref[0:1, :] * z2 + mcw_ref[1:2, :] * z1 + mcw_ref[2:3, :] * z)

    y_ref[0] = _merge_and_norm(x, xb, (attn, gm, scv), w_gate_ref, b_gate_ref,
                               (p_attn_ref, p_gmlp_ref, p_conv_ref), w_o_ref, ln1g_ref, ln1b_ref, alpha)


def _mixer_sample_kernel(sinks_ref, x_ref, ck_ref, cv_ref, mcs_ref, w_in_ref, w_gate_ref, b_gate_ref,
                         lng_ref, lnb_ref, wsbd_ref, bs8_ref, mcw_ref, p_attn_ref, p_gmlp_ref,
                         p_conv_ref, w_o_ref, ln1g_ref, ln1b_ref,
                         y_ref, knew_ref, vnew_ref, mc_ref, gvn_ref, *, alpha, n_heads, layer):
    bt, t, d = x_ref.shape
    kv_w = ck_ref.shape[2]
    q_w = n_heads * HEAD_DIM
    gm_w = lng_ref.shape[1]
    sc_w = mcw_ref.shape[1]
    rows = bt * t

    x = x_ref[...].reshape(rows, d)
    xb = x.astype(BF16)
    off = [0]

    def proj(width):
        lo = off[0]
        off[0] = lo + width
        return _dot(xb, w_in_ref[:, lo:lo + width])

    q = (proj(q_w) * (HEAD_DIM ** -0.5)).reshape(bt, t, q_w)
    k = proj(kv_w).reshape(bt, t, kv_w)
    v = proj(kv_w).reshape(bt, t, kv_w)
    gu = proj(gm_w)
    gv = proj(gm_w)
    sb = proj(sc_w)
    sc = proj(sc_w)
    sh = proj(sc_w)

    ck = ck_ref[...]
    cv = cv_ref[...]
    knew_ref[:, 0:WINDOW - t, :] = ck[:, t:, :]
    knew_ref[:, WINDOW - t:, :] = k
    vnew_ref[:, 0:WINDOW - t, :] = cv[:, t:, :]
    vnew_ref[:, WINDOW - t:, :] = v
    pad = jnp.zeros((bt, SAMPLE_KEYS - WINDOW - t, kv_w), F32)
    kkf = jnp.concatenate([ck, k, pad], axis=1)
    vvf = jnp.concatenate([cv, v, pad], axis=1)

    qrows = Q_GROUP * t
    ri = lax.broadcasted_iota(jnp.int32, (qrows, SAMPLE_KEYS), 0)
    ci = lax.broadcasted_iota(jnp.int32, (qrows, SAMPLE_KEYS), 1)
    dist = ri % t + WINDOW - ci
    valid = (dist >= 0) & (dist <= WINDOW) & (ci < WINDOW + t)
    distf = dist.astype(F32)
    gi = ri // t
    head_outs = []
    for kh in range(n_heads // Q_GROUP):
        kk = kkf[:, :, kh * HEAD_DIM:(kh + 1) * HEAD_DIM].astype(BF16)
        vv = vvf[:, :, kh * HEAD_DIM:(kh + 1) * HEAD_DIM].astype(BF16)
        qg = jnp.concatenate(
            [q[:, :, (kh * Q_GROUP + g) * HEAD_DIM:(kh * Q_GROUP + g + 1) * HEAD_DIM] for g in range(Q_GROUP)],
            axis=1).astype(BF16)
        s = jnp.einsum("bqd,bkd->bqk", qg, kk, preferred_element_type=F32)
        slope = jnp.zeros((qrows, SAMPLE_KEYS), F32)
        sink = jnp.zeros((qrows, 1), F32)
        for g in range(Q_GROUP):
            h = kh * Q_GROUP + g
            slope = jnp.where(gi == g, 2.0 ** (-8.0 * (h + 1) / n_heads), slope)
            sink = jnp.where(gi[:, 0:1] == g, sinks_ref[layer, h], sink)
        bias = jnp.where(valid, -slope * distf, -jnp.inf)
        o = _softmax_pv(s + bias[None], sink[None], vv)
        head_outs += [o[:, g * t:(g + 1) * t, :] for g in range(Q_GROUP)]
    attn = jnp.concatenate(head_outs, axis=2).reshape(rows, q_w)

    gvn = _layer_norm(gv, lng_ref[...], lnb_ref[...])
    gvn_ref[...] = gvn.reshape(bt, t, gm_w)
    gvb = gvn.astype(BF16)
    gw = gm_w // GM_GROUPS
    sv = jnp.concatenate([_dot(wsbd_ref[g], gvb[:, g * gw:(g + 1) * gw]) for g in range(GM_GROUPS)], axis=1)
    sv = sv.reshape(bt, t, gm_w) + bs8_ref[...][None]
    gm = gu * sv.reshape(rows, gm_w)

    z = (sc * sh).reshape(bt, t, sc_w)
    z1, z2 = _shift_rows_3d(mcs_ref[...], z)
    mc_ref[...] = z
    cz = mcw_ref[0:1, :][None] * z2 + mcw_ref[1:2, :][None] * z1 + mcw_ref[2:3, :][None] * z
    scv = sb * cz.reshape(rows, sc_w)

    y = _merge_and_norm(x, xb, (attn, gm, scv), w_gate_ref, b_gate_ref,
                        (p_attn_ref, p_gmlp_ref, p_conv_ref), w_o_ref, ln1g_ref, ln1b_ref, alpha)
    y_ref[...] = y.reshape(bt, t, d)


def _ffn_core(x, up_ref, cw_ref, cb_ref, w_down_ref, g_ref, b_ref, hbuf, shift, store_up, alpha):
    d_ff = w_down_ref.shape[0]
    xb = x.astype(BF16)
    for j in range(d_ff // FF_CHUNK):
        parts = []
        for base in (0, d_ff):
            lo = base + j * FF_CHUNK
            up = _dot(xb, up_ref[:, lo:lo + FF_CHUNK])
            u1, u2 = shift(up, lo)
            store_up(up, lo)
            parts.append(cw_ref[0:1, lo:lo + FF_CHUNK] * u2 + cw_ref[1:2, lo:lo + FF_CHUNK] * u1
                         + cw_ref[2:3, lo:lo + FF_CHUNK] * up + cb_ref[:, lo:lo + FF_CHUNK])
        a, g = parts
        hbuf[:, j * FF_CHUNK:(j + 1) * FF_CHUNK] = (g * _sigmoid(g) * a).astype(BF16)
    f = _dot(hbuf[...], w_down_ref[...])
    return _layer_norm(alpha * x + f, g_ref[...], b_ref[...])


def _ffn_prompt_kernel(x_ref, up_ref, cw_ref, cb_ref, w_down_ref, g_ref, b_ref,
                       y_ref, fc_ref, ubuf, hbuf, *, alpha):
    t = x_ref.shape[1]

    @pl.when(pl.program_id(1) == 0)
    def _():
        ubuf[...] = jnp.zeros_like(ubuf)

    def shift(up, lo):
        return _shift_rows_roll(ubuf[:, lo:lo + FF_CHUNK], up)

    def store_up(up, lo):
        ubuf[:, lo:lo + FF_CHUNK] = up[t - SUBLANES:]
        fc_ref[0, :, lo:lo + FF_CHUNK] = up[t - SUBLANES:]

    y_ref[0] = _ffn_core(x_ref[0], up_ref, cw_ref, cb_ref, w_down_ref, g_ref, b_ref, hbuf,
                         shift, store_up, alpha)


def _ffn_sample_kernel(x_ref, past_ref, up_ref, cw_ref, cb_ref, w_down_ref, g_ref, b_ref,
                       y_ref, fc_ref, hbuf, *, alpha):
    bt, t, d = x_ref.shape

    def shift(up, lo):
        u1, u2 = _shift_rows_3d(past_ref[:, :, lo:lo + FF_CHUNK], up.reshape(bt, t, FF_CHUNK))
        return u1.reshape(bt * t, FF_CHUNK), u2.reshape(bt * t, FF_CHUNK)

    def store_up(up, lo):
        fc_ref[:, :, lo:lo + FF_CHUNK] = up.reshape(bt, t, FF_CHUNK)

    y = _ffn_core(x_ref[...].reshape(bt * t, d), up_ref, cw_ref, cb_ref, w_down_ref, g_ref, b_ref, hbuf,
                  shift, store_up, alpha)
    y_ref[...] = y.reshape(bt, t, d)


def _resident(a, layer):
    idx = (layer,) + (0,) * (a.ndim - 1)
    return pl.BlockSpec((None,) + a.shape[1:], lambda *_: idx, pipeline_mode=pl.Buffered(1))


_MIXER_PROMPT_WEIGHTS = ("w_in", "w_gate", "b_gate", "lng", "lnb", "ws", "bst", "mcw",
                         "p_attn", "p_gmlp", "p_conv", "w_o", "ln1g", "ln1b")
_MIXER_SAMPLE_WEIGHTS = ("w_in", "w_gate", "b_gate", "lng", "lnb", "wsbd", "bs8", "mcw",
                         "p_attn", "p_gmlp", "p_conv", "w_o", "ln1g", "ln1b")
_FFN_WEIGHTS = ("w_up", "fcw", "fcb", "w_down", "ln2g", "ln2b")


def _params():
    return pltpu.CompilerParams(dimension_semantics=("arbitrary", "arbitrary"),
                                vmem_limit_bytes=VMEM_LIMIT_BYTES)


def _mixer_prompt(x, w, layer, alpha, n_heads):
    b, s, d = x.shape
    t = PROMPT_TILE
    kv_w = (n_heads // Q_GROUP) * HEAD_DIM
    sc_w = w["mcw"].shape[-1]
    weights = [w[k] for k in _MIXER_PROMPT_WEIGHTS]
    tile = pl.BlockSpec((1, t, d), lambda i, j: (i, j, 0))
    per_seq = lambda rows, width: pl.BlockSpec((1, rows, width), lambda i, j: (i, 0, 0))
    return pl.pallas_call(
        functools.partial(_mixer_prompt_kernel, alpha=alpha, n_heads=n_heads, layer=layer),
        grid=(b, s // t),
        in_specs=[pl.BlockSpec(memory_space=pltpu.SMEM), tile] + [_resident(a, layer) for a in weights],
        out_specs=[tile, per_seq(WINDOW, kv_w), per_seq(WINDOW, kv_w), per_seq(SUBLANES, sc_w)],
        out_shape=[jax.ShapeDtypeStruct((b, s, d), F32),
                   jax.ShapeDtypeStruct((b, WINDOW, kv_w), F32),
                   jax.ShapeDtypeStruct((b, WINDOW, kv_w), F32),
                   jax.ShapeDtypeStruct((b, SUBLANES, sc_w), F32)],
        scratch_shapes=[pltpu.VMEM((WINDOW, kv_w), F32), pltpu.VMEM((WINDOW, kv_w), F32),
                        pltpu.VMEM((SUBLANES, sc_w), F32)],
        compiler_params=_params(),
        name="mixer_prompt",
    )(w["sinks"], x, *weights)


def _mixer_sample(x, ck, cv, mcs, w, layer, alpha, n_heads):
    b, t, d = x.shape
    bt = SAMPLE_SEQS
    kv_w = ck.shape[-1]
    gm_w = w["lng"].shape[-1]
    sc_w = w["mcw"].shape[-1]
    weights = [w[k] for k in _MIXER_SAMPLE_WEIGHTS]
    seqs = lambda rows, width: pl.BlockSpec((bt, rows, width), lambda i, j: (i, 0, 0))
    state = lambda rows, width: pl.BlockSpec((None, bt, rows, width), lambda i, j: (layer, i, 0, 0))
    return pl.pallas_call(
        functools.partial(_mixer_sample_kernel, alpha=alpha, n_heads=n_heads, layer=layer),
        grid=(b // bt, 1),
        in_specs=[pl.BlockSpec(memory_space=pltpu.SMEM), seqs(t, d), state(WINDOW, kv_w), state(WINDOW, kv_w),
                  state(CONV_W - 1, sc_w)] + [_resident(a, layer) for a in weights],
        out_specs=[seqs(t, d), seqs(WINDOW, kv_w), seqs(WINDOW, kv_w), seqs(t, sc_w), seqs(t, gm_w)],
        out_shape=[jax.ShapeDtypeStruct((b, t, d), F32),
                   jax.ShapeDtypeStruct((b, WINDOW, kv_w), F32),
                   jax.ShapeDtypeStruct((b, WINDOW, kv_w), F32),
                   jax.ShapeDtypeStruct((b, t, sc_w), F32),
                   jax.ShapeDtypeStruct((b, t, gm_w), F32)],
        compiler_params=_params(),
        name="mixer_sample",
    )(w["sinks"], x, ck, cv, mcs, *weights)


def _ffn_prompt(x, w, layer, alpha):
    b, s, d = x.shape
    t = PROMPT_TILE
    d_ff = w["w_down"].shape[1]
    weights = [w[k] for k in _FFN_WEIGHTS]
    tile = pl.BlockSpec((1, t, d), lambda i, j: (i, j, 0))
    return pl.pallas_call(
        functools.partial(_ffn_prompt_kernel, alpha=alpha),
        grid=(b, s // t),
        in_specs=[tile] + [_resident(a, layer) for a in weights],
        out_specs=[tile, pl.BlockSpec((1, SUBLANES, 2 * d_ff), lambda i, j: (i, 0, 0))],
        out_shape=[jax.ShapeDtypeStruct((b, s, d), F32),
                   jax.ShapeDtypeStruct((b, SUBLANES, 2 * d_ff), F32)],
        scratch_shapes=[pltpu.VMEM((SUBLANES, 2 * d_ff), F32), pltpu.VMEM((t, d_ff), BF16)],
        compiler_params=_params(),
        name="ffn_prompt",
    )(x, *weights)


def _ffn_sample(x, past, w, layer, alpha):
    b, t, d = x.shape
    bt = SAMPLE_SEQS
    d_ff = w["w_down"].shape[1]
    weights = [w[k] for k in _FFN_WEIGHTS]
    seqs = lambda rows, width: pl.BlockSpec((bt, rows, width), lambda i, j: (i, 0, 0))
    return pl.pallas_call(
        functools.partial(_ffn_sample_kernel, alpha=alpha),
        grid=(b // bt, 1),
        in_specs=[seqs(t, d), pl.BlockSpec((None, bt, CONV_W - 1, 2 * d_ff), lambda i, j: (layer, i, 0, 0))]
        + [_resident(a, layer) for a in weights],
        out_specs=[seqs(t, d), seqs(t, 2 * d_ff)],
        out_shape=[jax.ShapeDtypeStruct((b, t, d), F32),
                   jax.ShapeDtypeStruct((b, t, 2 * d_ff), F32)],
        scratch_shapes=[pltpu.VMEM((bt * t, d_ff), BF16)],
        compiler_params=_params(),
        name="ffn_sample",
    )(x, past, *weights)


def kernel(x_prompt, x_sample, cache_k_win, cache_v_win, state_mixconv, state_ffnconv, w_in, w_gate, b_gate, gmlp_ln_g, gmlp_ln_b, gmlp_ws, gmlp_bs, mixconv_w, attn_sinks, p_attn, p_gmlp, p_conv, w_o, ln1_g, ln1_b, w_up, ffn_conv_w, ffn_conv_b, w_down, ln2_g, ln2_b):
    depth = w_in.shape[0]
    d = x_prompt.shape[-1]
    n_heads = d // 128
    kv_heads = n_heads // Q_GROUP
    alpha = (2.0 * depth) ** 0.25
    dec_b, dec_t = x_sample.shape[:2]
    gm_w = gmlp_ln_g.shape[1]
    gw = gm_w // GM_GROUPS
    rows_s = SAMPLE_SEQS * dec_t

    ws_t = jnp.tril(gmlp_ws[:, :, :dec_t, :dec_t])
    eye = jnp.eye(SAMPLE_SEQS, dtype=F32)
    wsbd = jnp.einsum("ab,lgts->lgatbs", eye, ws_t).reshape(depth, GM_GROUPS, rows_s, rows_s).astype(BF16)
    bs8 = jnp.repeat(jnp.swapaxes(gmlp_bs[:, :, :dec_t], 1, 2), gw, axis=2)

    row = lambda a: a[:, None, :]
    w = dict(
        sinks=attn_sinks,
        w_in=w_in.astype(BF16), w_gate=w_gate.astype(BF16), b_gate=row(b_gate),
        lng=row(gmlp_ln_g), lnb=row(gmlp_ln_b), ws=gmlp_ws, bst=jnp.swapaxes(gmlp_bs, 1, 2),
        wsbd=wsbd, bs8=bs8, mcw=mixconv_w,
        p_attn=p_attn.astype(BF16), p_gmlp=p_gmlp.astype(BF16), p_conv=p_conv.astype(BF16),
        w_o=w_o.astype(BF16), ln1g=row(ln1_g), ln1b=row(ln1_b),
        w_up=w_up.astype(BF16), fcw=ffn_conv_w, fcb=row(ffn_conv_b),
        w_down=w_down.astype(BF16), ln2g=row(ln2_g), ln2b=row(ln2_b))

    xp, xs = x_prompt, x_sample
    ck_all = cache_k_win.reshape(depth, dec_b, WINDOW, kv_heads * HEAD_DIM)
    cv_all = cache_v_win.reshape(depth, dec_b, WINDOW, kv_heads * HEAD_DIM)
    outs = {k: [] for k in ("kp", "vp", "mcp", "fcp", "ks", "vs", "mcs", "fcs", "gvs")}
    for l in range(depth):
        xp, kwin, vwin, mc = _mixer_prompt(xp, w, l, alpha, n_heads)
        xp, fc = _ffn_prompt(xp, w, l, alpha)
        xs, knew, vnew, mcn, gvn = _mixer_sample(xs, ck_all, cv_all, state_mixconv, w, l, alpha, n_heads)
        xs, fcn = _ffn_sample(xs, state_ffnconv, w, l, alpha)
        for key, val in zip(outs, (kwin, vwin, mc, fc, knew, vnew, mcn, fcn, gvn)):
            outs[key].append(val)
    st = {k: jnp.stack(v) for k, v in outs.items()}
    heads = lambda a: a.reshape(a.shape[:3] + (kv_heads, HEAD_DIM))
    last = lambda a: a[:, :, a.shape[2] - (CONV_W - 1):]
    return (xp, xs, heads(st["kp"]), heads(st["vp"]), last(st["mcp"]), last(st["fcp"]),
            heads(st["ks"]), heads(st["vs"]), last(st["mcs"]), last(st["fcs"]), st["gvs"])
```

```python
import functools

import jax
import jax.numpy as jnp
from jax import lax
from jax.experimental import pallas as pl
from jax.experimental.pallas import tpu as pltpu

F32 = jnp.float32
BF16 = jnp.bfloat16

HEAD_DIM = 64
WINDOW = 128
CHUNK = 128
GM_GROUPS = 4
CONV_W = 3
N_BRANCH = 3
LN_EPS = 1e-5
LOG2_E = 1.4426950408889634
Q_GROUP = 4

SUBLANES = 8
LANES = 128
VMEM_LIMIT_BYTES = 60 * 1024 * 1024

PROMPT_TILE = 512
SAMPLE_SEQS = 32
FF_CHUNK = 256
TAIL_SLABS = 4
SAMPLE_KEYS = 256


def _dot(a, b):
    return jnp.dot(a, b, preferred_element_type=F32)


def _layer_norm(x, g, b):
    mu = jnp.mean(x, axis=-1, keepdims=True)
    xc = x - mu
    var = jnp.mean(xc * xc, axis=-1, keepdims=True)
    return xc * lax.rsqrt(var + LN_EPS) * g + b


def _sigmoid(x):
    return 1.0 / (1.0 + jnp.exp2(x * (-LOG2_E)))


def _softmax_pv(s, sink, vv):
    m = jnp.maximum(jnp.max(s, axis=-1, keepdims=True), sink)
    p = jnp.exp2(s - m)
    denom = jnp.sum(p, axis=-1, keepdims=True) + jnp.exp2(sink - m)
    if s.ndim == 2:
        o = _dot(p.astype(BF16), vv)
    else:
        o = jnp.einsum("bqk,bkd->bqd", p.astype(BF16), vv, preferred_element_type=F32)
    return o / denom


def _shift_rows_roll(tail, cur):
    row = lax.broadcasted_iota(jnp.int32, tail.shape, 0)
    outs = []
    for k in (1, 2):
        r = pltpu.roll(cur, k, axis=0)
        head = jnp.where(row < k, pltpu.roll(tail, k, axis=0), r[:SUBLANES])
        outs.append(jnp.concatenate([head, r[SUBLANES:]], axis=0))
    return outs


def _shift_rows_3d(past, cur):
    t = lax.broadcasted_iota(jnp.int32, cur.shape, 1)
    p0 = jnp.broadcast_to(past[:, 0:1, :], cur.shape)
    p1 = jnp.broadcast_to(past[:, 1:2, :], cur.shape)
    r1 = pltpu.roll(cur, 1, axis=1)
    r2 = pltpu.roll(cur, 2, axis=1)
    s1 = jnp.where(t >= 1, r1, p1)
    s2 = jnp.where(t >= 2, r2, jnp.where(t == 1, p1, p0))
    return s1, s2


def _project_residual_norm(lhs, w_ref, x, g_ref, b_ref, alpha):
    rows = x.shape[0]
    slab = rows // TAIL_SLABS if rows % (TAIL_SLABS * 2 * SUBLANES) == 0 else rows
    outs = []
    for r in range(0, rows, slab):
        y = alpha * x[r:r + slab] + _dot(lhs[r:r + slab], w_ref[...])
        outs.append(_layer_norm(y, g_ref[...], b_ref[...]))
    return outs[0] if len(outs) == 1 else jnp.concatenate(outs, axis=0)


def _merge_and_norm(x, xb, branches, w_gate_ref, b_gate_ref, p_refs, w_o_ref, g_ref, b_ref, alpha):
    d = x.shape[-1]
    merged = None
    for i, (br, p_ref) in enumerate(zip(branches, p_refs)):
        gate = _sigmoid(_dot(xb, w_gate_ref[:, i * d:(i + 1) * d]) + b_gate_ref[:, i * d:(i + 1) * d])
        term = gate * _dot(br.astype(BF16), p_ref[...])
        merged = term if merged is None else merged + term
    return _project_residual_norm(merged.astype(BF16), w_o_ref, x, g_ref, b_ref, alpha)


def _mixer_prompt_kernel(sinks_ref, x_ref, w_in_ref, w_gate_ref, b_gate_ref, lng_ref, lnb_ref,
                         ws_ref, bst_ref, mcw_ref, p_attn_ref, p_gmlp_ref, p_conv_ref, w_o_ref,
                         ln1g_ref, ln1b_ref,
                         y_ref, kwin_ref, vwin_ref, mc_ref,
                         kprev, vprev, ztail, bias_sc, *, alpha, n_heads, layer):
    n = pl.program_id(1)
    t = x_ref.shape[1]
    kv_w = kprev.shape[1]
    q_w = n_heads * HEAD_DIM
    gm_w = lng_ref.shape[1]
    sc_w = mcw_ref.shape[1]

    @pl.when(n == 0)
    def _():
        kprev[...] = jnp.zeros_like(kprev)
        vprev[...] = jnp.zeros_like(vprev)
        ztail[...] = jnp.zeros_like(ztail)

    x = x_ref[0]
    xb = x.astype(BF16)
    off = [0]

    def proj(width):
        lo = off[0]
        off[0] = lo + width
        return _dot(xb, w_in_ref[:, lo:lo + width])

    q = proj(q_w) * (HEAD_DIM ** -0.5 * LOG2_E)
    k = proj(kv_w)
    v = proj(kv_w)
    gu = proj(gm_w)
    gv = proj(gm_w)
    sb = proj(sc_w)
    sc = proj(sc_w)
    sh = proj(sc_w)

    kext = jnp.concatenate([kprev[...], k], axis=0)
    vext = jnp.concatenate([vprev[...], v], axis=0)
    kprev[...] = k[t - WINDOW:]
    vprev[...] = v[t - WINDOW:]
    kwin_ref[0] = k[t - WINDOW:]
    vwin_ref[0] = v[t - WINDOW:]

    ii = lax.broadcasted_iota(jnp.int32, (WINDOW, 2 * WINDOW), 0)
    jj = lax.broadcasted_iota(jnp.int32, (WINDOW, 2 * WINDOW), 1)
    dist = ii + WINDOW - jj
    allowed = (dist >= 0) & (dist <= WINDOW)
    distf = dist.astype(F32)
    for h in range(n_heads):
        slope = 2.0 ** (-8.0 * (h + 1) / n_heads)
        bias_sc[h] = jnp.where(allowed, distf * (-slope * LOG2_E), -jnp.inf)
    first_ok = (jj >= WINDOW) | (n > 0)
    blocks = []
    for i in range(t // WINDOW):
        heads = []
        for kh in range(n_heads // Q_GROUP):
            kk = kext[i * WINDOW:(i + 2) * WINDOW, kh * HEAD_DIM:(kh + 1) * HEAD_DIM].astype(BF16)
            vv = vext[i * WINDOW:(i + 2) * WINDOW, kh * HEAD_DIM:(kh + 1) * HEAD_DIM].astype(BF16)
            for g in range(Q_GROUP):
                h = kh * Q_GROUP + g
                qh = q[i * WINDOW:(i + 1) * WINDOW, h * HEAD_DIM:(h + 1) * HEAD_DIM].astype(BF16)
                s = lax.dot_general(qh, kk, (((1,), (1,)), ((), ())), preferred_element_type=F32)
                bias = bias_sc[h]
                if i == 0:
                    bias = jnp.where(first_ok, bias, -jnp.inf)
                heads.append(_softmax_pv(s + bias, sinks_ref[layer, h] * LOG2_E, vv))
        blocks.append(jnp.concatenate(heads, axis=1))
    attn = jnp.concatenate(blocks, axis=0)

    gvn = _layer_norm(gv, lng_ref[...], lnb_ref[...]).astype(BF16)
    rr = lax.broadcasted_iota(jnp.int32, (CHUNK, CHUNK), 0)
    cc = lax.broadcasted_iota(jnp.int32, (CHUNK, CHUNK), 1)
    gw = gm_w // GM_GROUPS
    cols = []
    for g in range(GM_GROUPS):
        wsg = jnp.where(rr >= cc, ws_ref[g], 0.0).astype(BF16)
        bias = bst_ref[:, g:g + 1]
        rows = [_dot(wsg, gvn[c * CHUNK:(c + 1) * CHUNK, g * gw:(g + 1) * gw]) + bias
                for c in range(t // CHUNK)]
        cols.append(jnp.concatenate(rows, axis=0))
    gm = gu * jnp.concatenate(cols, axis=1)

    z = sc * sh
    z1, z2 = _shift_rows_roll(ztail[...], z)
    ztail[...] = z[t - SUBLANES:]
    mc_ref[0] = z[t - SUBLANES:]
    scv = sb * (mcw_ref[0:1, :] * z2 + mcw_ref[1:2, :] * z1 + mcw_ref[2:3, :] * z)

    y_ref[0] = _merge_and_norm(x, xb, (attn, gm, scv), w_gate_ref, b_gate_ref,
                               (p_attn_ref, p_gmlp_ref, p_conv_ref), w_o_ref, ln1g_ref, ln1b_ref, alpha)


def _mixer_sample_kernel(sinks_ref, x_ref, ck_ref, cv_ref, mcs_ref, w_in_ref, w_gate_ref, b_gate_ref,
                         lng_ref, lnb_ref, wsbd_ref, bs8_ref, mcw_ref, p_attn_ref, p_gmlp_ref,
                         p_conv_ref, w_o_ref, ln1g_ref, ln1b_ref,
                         y_ref, knew_ref, vnew_ref, mc_ref, gvn_ref, *, alpha, n_heads, layer):
    bt, t, d = x_ref.shape
    kv_w = ck_ref.shape[2]
    q_w = n_heads * HEAD_DIM
    gm_w = lng_ref.shape[1]
    sc_w = mcw_ref.shape[1]
    rows = bt * t

    x = x_ref[...].reshape(rows, d)
    xb = x.astype(BF16)
    off = [0]

    def proj(width):
        lo = off[0]
        off[0] = lo + width
        return _dot(xb, w_in_ref[:, lo:lo + width])

    q = (proj(q_w) * (HEAD_DIM ** -0.5 * LOG2_E)).reshape(bt, t, q_w)
    k = proj(kv_w).reshape(bt, t, kv_w)
    v = proj(kv_w).reshape(bt, t, kv_w)
    gu = proj(gm_w)
    gv = proj(gm_w)
    sb = proj(sc_w)
    sc = proj(sc_w)
    sh = proj(sc_w)

    ck = ck_ref[...]
    cv = cv_ref[...]
    knew_ref[:, 0:WINDOW - t, :] = ck[:, t:, :]
    knew_ref[:, WINDOW - t:, :] = k
    vnew_ref[:, 0:WINDOW - t, :] = cv[:, t:, :]
    vnew_ref[:, WINDOW - t:, :] = v
    pad = jnp.zeros((bt, SAMPLE_KEYS - WINDOW - t, kv_w), F32)
    kkf = jnp.concatenate([ck, k, pad], axis=1)
    vvf = jnp.concatenate([cv, v, pad], axis=1)

    qrows = Q_GROUP * t
    ri = lax.broadcasted_iota(jnp.int32, (qrows, SAMPLE_KEYS), 0)
    ci = lax.broadcasted_iota(jnp.int32, (qrows, SAMPLE_KEYS), 1)
    dist = ri % t + WINDOW - ci
    valid = (dist >= 0) & (dist <= WINDOW) & (ci < WINDOW + t)
    distf = dist.astype(F32)
    gi = ri // t
    head_outs = []
    for kh in range(n_heads // Q_GROUP):
        kk = kkf[:, :, kh * HEAD_DIM:(kh + 1) * HEAD_DIM].astype(BF16)
        vv = vvf[:, :, kh * HEAD_DIM:(kh + 1) * HEAD_DIM].astype(BF16)
        qg = jnp.concatenate(
            [q[:, :, (kh * Q_GROUP + g) * HEAD_DIM:(kh * Q_GROUP + g + 1) * HEAD_DIM] for g in range(Q_GROUP)],
            axis=1).astype(BF16)
        s = jnp.einsum("bqd,bkd->bqk", qg, kk, preferred_element_type=F32)
        slope = jnp.zeros((qrows, SAMPLE_KEYS), F32)
        sink = jnp.zeros((qrows, 1), F32)
        for g in range(Q_GROUP):
            h = kh * Q_GROUP + g
            slope = jnp.where(gi == g, 2.0 ** (-8.0 * (h + 1) / n_heads) * LOG2_E, slope)
            sink = jnp.where(gi[:, 0:1] == g, sinks_ref[layer, h] * LOG2_E, sink)
        bias = jnp.where(valid, -slope * distf, -jnp.inf)
        o = _softmax_pv(s + bias[None], sink[None], vv)
        head_outs += [o[:, g * t:(g + 1) * t, :] for g in range(Q_GROUP)]
    attn = jnp.concatenate(head_outs, axis=2).reshape(rows, q_w)

    gvn = _layer_norm(gv, lng_ref[...], lnb_ref[...])
    gvn_ref[...] = gvn.reshape(bt, t, gm_w)
    gvb = gvn.astype(BF16)
    gw = gm_w // GM_GROUPS
    sv = jnp.concatenate([_dot(wsbd_ref[g], gvb[:, g * gw:(g + 1) * gw]) for g in range(GM_GROUPS)], axis=1)
    sv = sv.reshape(bt, t, gm_w) + bs8_ref[...][None]
    gm = gu * sv.reshape(rows, gm_w)

    z = (sc * sh).reshape(bt, t, sc_w)
    z1, z2 = _shift_rows_3d(mcs_ref[...], z)
    mc_ref[...] = z[:, t - (CONV_W - 1):, :]
    cz = mcw_ref[0:1, :][None] * z2 + mcw_ref[1:2, :][None] * z1 + mcw_ref[2:3, :][None] * z
    scv = sb * cz.reshape(rows, sc_w)

    y = _merge_and_norm(x, xb, (attn, gm, scv), w_gate_ref, b_gate_ref,
                        (p_attn_ref, p_gmlp_ref, p_conv_ref), w_o_ref, ln1g_ref, ln1b_ref, alpha)
    y_ref[...] = y.reshape(bt, t, d)


def _ffn_hidden(x, up_ref, cw_ref, cb_ref, shift, store_up):
    d_ff = up_ref.shape[1] // 2
    xb = x.astype(BF16)
    hs = []
    for j in range(d_ff // FF_CHUNK):
        parts = []
        for base in (0, d_ff):
            lo = base + j * FF_CHUNK
            up = _dot(xb, up_ref[:, lo:lo + FF_CHUNK])
            u1, u2 = shift(up, lo)
            store_up(up, lo)
            parts.append(cw_ref[0:1, lo:lo + FF_CHUNK] * u2 + cw_ref[1:2, lo:lo + FF_CHUNK] * u1
                         + cw_ref[2:3, lo:lo + FF_CHUNK] * up + cb_ref[:, lo:lo + FF_CHUNK])
        a, g = parts
        hs.append((g * _sigmoid(g) * a).astype(BF16))
    return jnp.concatenate(hs, axis=1)


def _ffn_out(x, h, w_down_ref, g_ref, b_ref, alpha):
    return _project_residual_norm(h, w_down_ref, x, g_ref, b_ref, alpha)


def _ffn_prompt_kernel(x_ref, up_ref, cw_ref, cb_ref, w_down_ref, g_ref, b_ref,
                       y_ref, fc_ref, ubuf, *, alpha):
    t = x_ref.shape[1]

    @pl.when(pl.program_id(1) == 0)
    def _():
        ubuf[...] = jnp.zeros_like(ubuf)

    def shift(up, lo):
        return _shift_rows_roll(ubuf[:, lo:lo + FF_CHUNK], up)

    def store_up(up, lo):
        ubuf[:, lo:lo + FF_CHUNK] = up[t - SUBLANES:]
        fc_ref[0, :, lo:lo + FF_CHUNK] = up[t - SUBLANES:]

    x = x_ref[0]
    h = _ffn_hidden(x, up_ref, cw_ref, cb_ref, shift, store_up)
    y_ref[0] = _ffn_out(x, h, w_down_ref, g_ref, b_ref, alpha)


def _ffn_sample_kernel(x_ref, past_ref, up_ref, cw_ref, cb_ref, w_down_ref, g_ref, b_ref,
                       y_ref, fc_ref, *, alpha):
    bt, t, d = x_ref.shape

    def shift(up, lo):
        u1, u2 = _shift_rows_3d(past_ref[:, :, lo:lo + FF_CHUNK], up.reshape(bt, t, FF_CHUNK))
        return u1.reshape(bt * t, FF_CHUNK), u2.reshape(bt * t, FF_CHUNK)

    def store_up(up, lo):
        fc_ref[:, :, lo:lo + FF_CHUNK] = up.reshape(bt, t, FF_CHUNK)[:, t - (CONV_W - 1):, :]

    x = x_ref[...].reshape(bt * t, d)
    h = _ffn_hidden(x, up_ref, cw_ref, cb_ref, shift, store_up)
    y = _ffn_out(x, h, w_down_ref, g_ref, b_ref, alpha)
    y_ref[...] = y.reshape(bt, t, d)


def _resident(a, layer):
    idx = (layer,) + (0,) * (a.ndim - 1)
    return pl.BlockSpec((None,) + a.shape[1:], lambda *_: idx, pipeline_mode=pl.Buffered(1))


_MIXER_PROMPT_WEIGHTS = ("w_in", "w_gate", "b_gate", "lng", "lnb", "ws", "bst", "mcw",
                         "p_attn", "p_gmlp", "p_conv", "w_o", "ln1g", "ln1b")
_MIXER_SAMPLE_WEIGHTS = ("w_in", "w_gate", "b_gate", "lng", "lnb", "wsbd", "bs8", "mcw",
                         "p_attn", "p_gmlp", "p_conv", "w_o", "ln1g", "ln1b")
_FFN_WEIGHTS = ("w_up", "fcw", "fcb", "w_down", "ln2g", "ln2b")


def _params():
    return pltpu.CompilerParams(dimension_semantics=("arbitrary", "arbitrary"),
                                vmem_limit_bytes=VMEM_LIMIT_BYTES)


def _mixer_prompt(x, w, layer, alpha, n_heads):
    b, s, d = x.shape
    t = PROMPT_TILE
    kv_w = (n_heads // Q_GROUP) * HEAD_DIM
    sc_w = w["mcw"].shape[-1]
    weights = [w[k] for k in _MIXER_PROMPT_WEIGHTS]
    tile = pl.BlockSpec((1, t, d), lambda i, j: (i, j, 0))
    per_seq = lambda rows, width: pl.BlockSpec((1, rows, width), lambda i, j: (i, 0, 0))
    return pl.pallas_call(
        functools.partial(_mixer_prompt_kernel, alpha=alpha, n_heads=n_heads, layer=layer),
        grid=(b, s // t),
        in_specs=[pl.BlockSpec(memory_space=pltpu.SMEM), tile] + [_resident(a, layer) for a in weights],
        out_specs=[tile, per_seq(WINDOW, kv_w), per_seq(WINDOW, kv_w), per_seq(SUBLANES, sc_w)],
        out_shape=[jax.ShapeDtypeStruct((b, s, d), F32),
                   jax.ShapeDtypeStruct((b, WINDOW, kv_w), F32),
                   jax.ShapeDtypeStruct((b, WINDOW, kv_w), F32),
                   jax.ShapeDtypeStruct((b, SUBLANES, sc_w), F32)],
        scratch_shapes=[pltpu.VMEM((WINDOW, kv_w), F32), pltpu.VMEM((WINDOW, kv_w), F32),
                        pltpu.VMEM((SUBLANES, sc_w), F32),
                        pltpu.VMEM((n_heads, WINDOW, 2 * WINDOW), F32)],
        compiler_params=_params(),
        name="mixer_prompt",
    )(w["sinks"], x, *weights)


def _mixer_sample(x, ck, cv, mcs, w, layer, alpha, n_heads):
    b, t, d = x.shape
    bt = SAMPLE_SEQS
    kv_w = ck.shape[-1]
    gm_w = w["lng"].shape[-1]
    sc_w = w["mcw"].shape[-1]
    weights = [w[k] for k in _MIXER_SAMPLE_WEIGHTS]
    seqs = lambda rows, width: pl.BlockSpec((bt, rows, width), lambda i, j: (i, 0, 0))
    state = lambda rows, width: pl.BlockSpec((None, bt, rows, width), lambda i, j: (layer, i, 0, 0))
    return pl.pallas_call(
        functools.partial(_mixer_sample_kernel, alpha=alpha, n_heads=n_heads, layer=layer),
        grid=(b // bt, 1),
        in_specs=[pl.BlockSpec(memory_space=pltpu.SMEM), seqs(t, d), state(WINDOW, kv_w), state(WINDOW, kv_w),
                  state(CONV_W - 1, sc_w)] + [_resident(a, layer) for a in weights],
        out_specs=[seqs(t, d), seqs(WINDOW, kv_w), seqs(WINDOW, kv_w), seqs(CONV_W - 1, sc_w), seqs(t, gm_w)],
        out_shape=[jax.ShapeDtypeStruct((b, t, d), F32),
                   jax.ShapeDtypeStruct((b, WINDOW, kv_w), F32),
                   jax.ShapeDtypeStruct((b, WINDOW, kv_w), F32),
                   jax.ShapeDtypeStruct((b, CONV_W - 1, sc_w), F32),
                   jax.ShapeDtypeStruct((b, t, gm_w), F32)],
        compiler_params=_params(),
        name="mixer_sample",
    )(w["sinks"], x, ck, cv, mcs, *weights)


def _ffn_prompt(x, w, layer, alpha):
    b, s, d = x.shape
    t = PROMPT_TILE
    d_ff = w["w_down"].shape[1]
    weights = [w[k] for k in _FFN_WEIGHTS]
    tile = pl.BlockSpec((1, t, d), lambda i, j: (i, j, 0))
    return pl.pallas_call(
        functools.partial(_ffn_prompt_kernel, alpha=alpha),
        grid=(b, s // t),
        in_specs=[tile] + [_resident(a, layer) for a in weights],
        out_specs=[tile, pl.BlockSpec((1, SUBLANES, 2 * d_ff), lambda i, j: (i, 0, 0))],
        out_shape=[jax.ShapeDtypeStruct((b, s, d), F32),
                   jax.ShapeDtypeStruct((b, SUBLANES, 2 * d_ff), F32)],
        scratch_shapes=[pltpu.VMEM((SUBLANES, 2 * d_ff), F32)],
        compiler_params=_params(),
        name="ffn_prompt",
    )(x, *weights)


def _ffn_sample(x, past, w, layer, alpha):
    b, t, d = x.shape
    bt = SAMPLE_SEQS
    d_ff = w["w_down"].shape[1]
    weights = [w[k] for k in _FFN_WEIGHTS]
    seqs = lambda rows, width: pl.BlockSpec((bt, rows, width), lambda i, j: (i, 0, 0))
    return pl.pallas_call(
        functools.partial(_ffn_sample_kernel, alpha=alpha),
        grid=(b // bt, 1),
        in_specs=[seqs(t, d), pl.BlockSpec((None, bt, CONV_W - 1, 2 * d_ff), lambda i, j: (layer, i, 0, 0))]
        + [_resident(a, layer) for a in weights],
        out_specs=[seqs(t, d), seqs(CONV_W - 1, 2 * d_ff)],
        out_shape=[jax.ShapeDtypeStruct((b, t, d), F32),
                   jax.ShapeDtypeStruct((b, CONV_W - 1, 2 * d_ff), F32)],
        compiler_params=_params(),
        name="ffn_sample",
    )(x, past, *weights)


def kernel(x_prompt, x_sample, cache_k_win, cache_v_win, state_mixconv, state_ffnconv, w_in, w_gate, b_gate, gmlp_ln_g, gmlp_ln_b, gmlp_ws, gmlp_bs, mixconv_w, attn_sinks, p_attn, p_gmlp, p_conv, w_o, ln1_g, ln1_b, w_up, ffn_conv_w, ffn_conv_b, w_down, ln2_g, ln2_b):
    depth = w_in.shape[0]
    d = x_prompt.shape[-1]
    n_heads = d // 128
    kv_heads = n_heads // Q_GROUP
    alpha = (2.0 * depth) ** 0.25
    dec_b, dec_t = x_sample.shape[:2]
    gm_w = gmlp_ln_g.shape[1]
    gw = gm_w // GM_GROUPS
    rows_s = SAMPLE_SEQS * dec_t

    ws_t = jnp.tril(gmlp_ws[:, :, :dec_t, :dec_t])
    eye = jnp.eye(SAMPLE_SEQS, dtype=F32)
    wsbd = jnp.einsum("ab,lgts->lgatbs", eye, ws_t).reshape(depth, GM_GROUPS, rows_s, rows_s).astype(BF16)
    bs8 = jnp.repeat(jnp.swapaxes(gmlp_bs[:, :, :dec_t], 1, 2), gw, axis=2)

    row = lambda a: a[:, None, :]
    w = dict(
        sinks=attn_sinks,
        w_in=w_in.astype(BF16), w_gate=w_gate.astype(BF16), b_gate=row(b_gate),
        lng=row(gmlp_ln_g), lnb=row(gmlp_ln_b), ws=gmlp_ws, bst=jnp.swapaxes(gmlp_bs, 1, 2),
        wsbd=wsbd, bs8=bs8, mcw=mixconv_w,
        p_attn=p_attn.astype(BF16), p_gmlp=p_gmlp.astype(BF16), p_conv=p_conv.astype(BF16),
        w_o=w_o.astype(BF16), ln1g=row(ln1_g), ln1b=row(ln1_b),
        w_up=w_up.astype(BF16), fcw=ffn_conv_w, fcb=row(ffn_conv_b),
        w_down=w_down.astype(BF16), ln2g=row(ln2_g), ln2b=row(ln2_b))

    xp, xs = x_prompt, x_sample
    ck_all = cache_k_win.reshape(depth, dec_b, WINDOW, kv_heads * HEAD_DIM)
    cv_all = cache_v_win.reshape(depth, dec_b, WINDOW, kv_heads * HEAD_DIM)
    outs = {k: [] for k in ("kp", "vp", "mcp", "fcp", "ks", "vs", "mcs", "fcs", "gvs")}
    for l in range(depth):
        xp, kwin, vwin, mc = _mixer_prompt(xp, w, l, alpha, n_heads)
        xp, fc = _ffn_prompt(xp, w, l, alpha)
        xs, knew, vnew, mcn, gvn = _mixer_sample(xs, ck_all, cv_all, state_mixconv, w, l, alpha, n_heads)
        xs, fcn = _ffn_sample(xs, state_ffnconv, w, l, alpha)
        for key, val in zip(outs, (kwin, vwin, mc, fc, knew, vnew, mcn, fcn, gvn)):
            outs[key].append(val)
    st = {k: jnp.stack(v) for k, v in outs.items()}
    heads = lambda a: a.reshape(a.shape[:3] + (kv_heads, HEAD_DIM))
    last = lambda a: a[:, :, a.shape[2] - (CONV_W - 1):]
    return (xp, xs, heads(st["kp"]), heads(st["vp"]), last(st["mcp"]), last(st["fcp"]),
            heads(st["ks"]), heads(st["vs"]), st["mcs"], st["fcs"], st["gvs"])
```

```python
import functools

import jax
import jax.numpy as jnp
from jax import lax
from jax.experimental import pallas as pl
from jax.experimental.pallas import tpu as pltpu

F32 = jnp.float32
BF16 = jnp.bfloat16

HEAD_DIM = 64
WINDOW = 128
CHUNK = 128
GM_GROUPS = 4
CONV_W = 3
N_BRANCH = 3
LN_EPS = 1e-5
LOG2_E = 1.4426950408889634
Q_GROUP = 4

SUBLANES = 8
LANES = 128
VMEM_LIMIT_BYTES = 60 * 1024 * 1024

PROMPT_TILE = 1024
FFN_TILE = 1024
SAMPLE_SEQS = 32
FF_CHUNK = 256
TAIL_SLABS = 4
SAMPLE_KEYS = 256


def _dot(a, b):
    return jnp.dot(a, b, preferred_element_type=F32)


def _layer_norm(x, g, b):
    mu = jnp.mean(x, axis=-1, keepdims=True)
    xc = x - mu
    var = jnp.mean(xc * xc, axis=-1, keepdims=True)
    return xc * lax.rsqrt(var + LN_EPS) * g + b


def _sigmoid(x):
    return 1.0 / (1.0 + jnp.exp2(x * (-LOG2_E)))


def _softmax_weights(s, sink):
    m = jnp.maximum(jnp.max(s, axis=-1, keepdims=True), sink)
    p = jnp.exp2(s - m)
    denom = jnp.sum(p, axis=-1, keepdims=True) + jnp.exp2(sink - m)
    return p.astype(BF16), denom


def _shift_rows_roll(tail, cur):
    row = lax.broadcasted_iota(jnp.int32, tail.shape, 0)
    outs = []
    for k in (1, 2):
        r = pltpu.roll(cur, k, axis=0)
        head = jnp.where(row < k, pltpu.roll(tail, k, axis=0), r[:SUBLANES])
        outs.append(jnp.concatenate([head, r[SUBLANES:]], axis=0))
    return outs


def _shift_rows_3d(past, cur):
    t = lax.broadcasted_iota(jnp.int32, cur.shape, 1)
    p0 = jnp.broadcast_to(past[:, 0:1, :], cur.shape)
    p1 = jnp.broadcast_to(past[:, 1:2, :], cur.shape)
    r1 = pltpu.roll(cur, 1, axis=1)
    r2 = pltpu.roll(cur, 2, axis=1)
    s1 = jnp.where(t >= 1, r1, p1)
    s2 = jnp.where(t >= 2, r2, jnp.where(t == 1, p1, p0))
    return s1, s2


def _project_residual_norm(lhs, w_ref, x, g_ref, b_ref, alpha):
    rows = x.shape[0]
    slab = rows // TAIL_SLABS if rows % (TAIL_SLABS * 2 * SUBLANES) == 0 else rows
    outs = []
    for r in range(0, rows, slab):
        y = alpha * x[r:r + slab] + _dot(lhs[r:r + slab], w_ref[...])
        outs.append(_layer_norm(y, g_ref[...], b_ref[...]))
    return outs[0] if len(outs) == 1 else jnp.concatenate(outs, axis=0)


def _merge_and_norm(x, xb, branches, w_gate_ref, b_gate_ref, p_refs, w_o_ref, g_ref, b_ref, alpha):
    d = x.shape[-1]
    merged = None
    for i, (br, p_ref) in enumerate(zip(branches, p_refs)):
        gate = _sigmoid(_dot(xb, w_gate_ref[:, i * d:(i + 1) * d]) + b_gate_ref[:, i * d:(i + 1) * d])
        term = gate * _dot(br.astype(BF16), p_ref[...])
        merged = term if merged is None else merged + term
    return _project_residual_norm(merged.astype(BF16), w_o_ref, x, g_ref, b_ref, alpha)


def _mixer_prompt_kernel(sinks_ref, x_ref, w_in_ref, w_gate_ref, b_gate_ref, lng_ref, lnb_ref,
                         ws_ref, bst_ref, mcw_ref, p_attn_ref, p_gmlp_ref, p_conv_ref, w_o_ref,
                         ln1g_ref, ln1b_ref,
                         y_ref, kwin_ref, vwin_ref, mc_ref,
                         kprev, vprev, ztail, bias_sc, *, alpha, n_heads, layer):
    n = pl.program_id(1)
    t = x_ref.shape[1]
    kv_w = kprev.shape[1]
    q_w = n_heads * HEAD_DIM
    gm_w = lng_ref.shape[1]
    sc_w = mcw_ref.shape[1]

    @pl.when(n == 0)
    def _():
        kprev[...] = jnp.zeros_like(kprev)
        vprev[...] = jnp.zeros_like(vprev)
        ztail[...] = jnp.zeros_like(ztail)

    x = x_ref[0]
    xb = x.astype(BF16)
    off = [0]

    def proj(width):
        lo = off[0]
        off[0] = lo + width
        return _dot(xb, w_in_ref[:, lo:lo + width])

    q = proj(q_w) * (HEAD_DIM ** -0.5 * LOG2_E)
    k = proj(kv_w)
    v = proj(kv_w)
    gu = proj(gm_w)
    gv = proj(gm_w)
    sb = proj(sc_w)
    sc = proj(sc_w)
    sh = proj(sc_w)

    kext = jnp.concatenate([kprev[...], k], axis=0)
    vext = jnp.concatenate([vprev[...], v], axis=0)
    kprev[...] = k[t - WINDOW:]
    vprev[...] = v[t - WINDOW:]
    kwin_ref[0] = k[t - WINDOW:]
    vwin_ref[0] = v[t - WINDOW:]

    ii = lax.broadcasted_iota(jnp.int32, (WINDOW, 2 * WINDOW), 0)
    jj = lax.broadcasted_iota(jnp.int32, (WINDOW, 2 * WINDOW), 1)
    dist = ii + WINDOW - jj
    allowed = (dist >= 0) & (dist <= WINDOW)
    distf = dist.astype(F32)
    for h in range(n_heads):
        slope = 2.0 ** (-8.0 * (h + 1) / n_heads)
        bias_sc[h] = jnp.where(allowed, distf * (-slope * LOG2_E), -jnp.inf)
    first_ok = (jj >= WINDOW) | (n > 0)
    blocks = []
    for i in range(t // WINDOW):
        heads = []
        for kh in range(n_heads // Q_GROUP):
            kk = kext[i * WINDOW:(i + 2) * WINDOW, kh * HEAD_DIM:(kh + 1) * HEAD_DIM].astype(BF16)
            vv = vext[i * WINDOW:(i + 2) * WINDOW, kh * HEAD_DIM:(kh + 1) * HEAD_DIM].astype(BF16)
            hs = range(kh * Q_GROUP, (kh + 1) * Q_GROUP)
            qg = jnp.concatenate([q[i * WINDOW:(i + 1) * WINDOW, h * HEAD_DIM:(h + 1) * HEAD_DIM]
                                  for h in hs], axis=0).astype(BF16)
            s_all = lax.dot_general(qg, kk, (((1,), (1,)), ((), ())), preferred_element_type=F32)
            ps, denoms = [], []
            for g, h in enumerate(hs):
                bias = bias_sc[h]
                if i == 0:
                    bias = jnp.where(first_ok, bias, -jnp.inf)
                p, denom = _softmax_weights(s_all[g * WINDOW:(g + 1) * WINDOW] + bias,
                                            sinks_ref[layer, h] * LOG2_E)
                ps.append(p)
                denoms.append(denom)
            o_all = _dot(jnp.concatenate(ps, axis=0), vv)
            heads += [o_all[g * WINDOW:(g + 1) * WINDOW] / denoms[g] for g in range(Q_GROUP)]
        blocks.append(jnp.concatenate(heads, axis=1))
    attn = jnp.concatenate(blocks, axis=0)

    gvn = _layer_norm(gv, lng_ref[...], lnb_ref[...]).astype(BF16)
    rr = lax.broadcasted_iota(jnp.int32, (CHUNK, CHUNK), 0)
    cc = lax.broadcasted_iota(jnp.int32, (CHUNK, CHUNK), 1)
    gw = gm_w // GM_GROUPS
    cols = []
    for g in range(GM_GROUPS):
        wsg = jnp.where(rr >= cc, ws_ref[g], 0.0).astype(BF16)
        bias = bst_ref[:, g:g + 1]
        rows = [_dot(wsg, gvn[c * CHUNK:(c + 1) * CHUNK, g * gw:(g + 1) * gw]) + bias
                for c in range(t // CHUNK)]
        cols.append(jnp.concatenate(rows, axis=0))
    gm = gu * jnp.concatenate(cols, axis=1)

    z = sc * sh
    z1, z2 = _shift_rows_roll(ztail[...], z)
    ztail[...] = z[t - SUBLANES:]
    mc_ref[0] = z[t - SUBLANES:]
    scv = sb * (mcw_ref[0:1, :] * z2 + mcw_ref[1:2, :] * z1 + mcw_ref[2:3, :] * z)

    y_ref[0] = _merge_and_norm(x, xb, (attn, gm, scv), w_gate_ref, b_gate_ref,
                               (p_attn_ref, p_gmlp_ref, p_conv_ref), w_o_ref, ln1g_ref, ln1b_ref, alpha)


def _mixer_sample_kernel(sinks_ref, x_ref, ck_ref, cv_ref, mcs_ref, w_in_ref, w_gate_ref, b_gate_ref,
                         lng_ref, lnb_ref, wsbd_ref, bs8_ref, mcw_ref, p_attn_ref, p_gmlp_ref,
                         p_conv_ref, w_o_ref, ln1g_ref, ln1b_ref,
                         y_ref, knew_ref, vnew_ref, mc_ref, gvn_ref, *, alpha, n_heads, layer):
    bt, t, d = x_ref.shape
    kv_w = ck_ref.shape[2]
    q_w = n_heads * HEAD_DIM
    gm_w = lng_ref.shape[1]
    sc_w = mcw_ref.shape[1]
    rows = bt * t

    x = x_ref[...].reshape(rows, d)
    xb = x.astype(BF16)
    off = [0]

    def proj(width):
        lo = off[0]
        off[0] = lo + width
        return _dot(xb, w_in_ref[:, lo:lo + width])

    q = (proj(q_w) * (HEAD_DIM ** -0.5 * LOG2_E)).reshape(bt, t, q_w)
    k = proj(kv_w).reshape(bt, t, kv_w)
    v = proj(kv_w).reshape(bt, t, kv_w)
    gu = proj(gm_w)
    gv = proj(gm_w)
    sb = proj(sc_w)
    sc = proj(sc_w)
    sh = proj(sc_w)

    ck = ck_ref[...]
    cv = cv_ref[...]
    knew_ref[:, 0:WINDOW - t, :] = ck[:, t:, :]
    knew_ref[:, WINDOW - t:, :] = k
    vnew_ref[:, 0:WINDOW - t, :] = cv[:, t:, :]
    vnew_ref[:, WINDOW - t:, :] = v
    pad = jnp.zeros((bt, SAMPLE_KEYS - WINDOW - t, kv_w), F32)
    kkf = jnp.concatenate([ck, k, pad], axis=1)
    vvf = jnp.concatenate([cv, v, pad], axis=1)

    qrows = Q_GROUP * t
    ri = lax.broadcasted_iota(jnp.int32, (qrows, SAMPLE_KEYS), 0)
    ci = lax.broadcasted_iota(jnp.int32, (qrows, SAMPLE_KEYS), 1)
    dist = ri % t + WINDOW - ci
    valid = (dist >= 0) & (dist <= WINDOW) & (ci < WINDOW + t)
    distf = dist.astype(F32)
    gi = ri // t
    head_outs = []
    for kh in range(n_heads // Q_GROUP):
        kk = kkf[:, :, kh * HEAD_DIM:(kh + 1) * HEAD_DIM].astype(BF16)
        vv = vvf[:, :, kh * HEAD_DIM:(kh + 1) * HEAD_DIM].astype(BF16)
        qg = jnp.concatenate(
            [q[:, :, (kh * Q_GROUP + g) * HEAD_DIM:(kh * Q_GROUP + g + 1) * HEAD_DIM] for g in range(Q_GROUP)],
            axis=1).astype(BF16)
        s = jnp.einsum("bqd,bkd->bqk", qg, kk, preferred_element_type=F32)
        slope = jnp.zeros((qrows, SAMPLE_KEYS), F32)
        sink = jnp.zeros((qrows, 1), F32)
        for g in range(Q_GROUP):
            h = kh * Q_GROUP + g
            slope = jnp.where(gi == g, 2.0 ** (-8.0 * (h + 1) / n_heads) * LOG2_E, slope)
            sink = jnp.where(gi[:, 0:1] == g, sinks_ref[layer, h] * LOG2_E, sink)
        bias = jnp.where(valid, -slope * distf, -jnp.inf)
        p, denom = _softmax_weights(s + bias[None], sink[None])
        o = jnp.einsum("bqk,bkd->bqd", p, vv, preferred_element_type=F32) / denom
        head_outs += [o[:, g * t:(g + 1) * t, :] for g in range(Q_GROUP)]
    attn = jnp.concatenate(head_outs, axis=2).reshape(rows, q_w)

    gvn = _layer_norm(gv, lng_ref[...], lnb_ref[...])
    gvn_ref[...] = gvn.reshape(bt, t, gm_w)
    gvb = gvn.astype(BF16)
    gw = gm_w // GM_GROUPS
    sv = jnp.concatenate([_dot(wsbd_ref[g], gvb[:, g * gw:(g + 1) * gw]) for g in range(GM_GROUPS)], axis=1)
    sv = sv.reshape(bt, t, gm_w) + bs8_ref[...][None]
    gm = gu * sv.reshape(rows, gm_w)

    z = (sc * sh).reshape(bt, t, sc_w)
    z1, z2 = _shift_rows_3d(mcs_ref[...], z)
    mc_ref[...] = z[:, t - (CONV_W - 1):, :]
    cz = mcw_ref[0:1, :][None] * z2 + mcw_ref[1:2, :][None] * z1 + mcw_ref[2:3, :][None] * z
    scv = sb * cz.reshape(rows, sc_w)

    y = _merge_and_norm(x, xb, (attn, gm, scv), w_gate_ref, b_gate_ref,
                        (p_attn_ref, p_gmlp_ref, p_conv_ref), w_o_ref, ln1g_ref, ln1b_ref, alpha)
    y_ref[...] = y.reshape(bt, t, d)


def _ffn_hidden(x, up_ref, cw_ref, cb_ref, shift, store_up):
    d_ff = up_ref.shape[1] // 2
    xb = x.astype(BF16)
    hs = []
    for j in range(d_ff // FF_CHUNK):
        parts = []
        for base in (0, d_ff):
            lo = base + j * FF_CHUNK
            up = _dot(xb, up_ref[:, lo:lo + FF_CHUNK])
            u1, u2 = shift(up, lo)
            store_up(up, lo)
            parts.append(cw_ref[0:1, lo:lo + FF_CHUNK] * u2 + cw_ref[1:2, lo:lo + FF_CHUNK] * u1
                         + cw_ref[2:3, lo:lo + FF_CHUNK] * up + cb_ref[:, lo:lo + FF_CHUNK])
        a, g = parts
        hs.append((g * _sigmoid(g) * a).astype(BF16))
    return jnp.concatenate(hs, axis=1)


def _ffn_out(x, h, w_down_ref, g_ref, b_ref, alpha):
    return _project_residual_norm(h, w_down_ref, x, g_ref, b_ref, alpha)


def _ffn_prompt_kernel(x_ref, up_ref, cw_ref, cb_ref, w_down_ref, g_ref, b_ref,
                       y_ref, fc_ref, ubuf, *, alpha):
    t = x_ref.shape[1]

    @pl.when(pl.program_id(1) == 0)
    def _():
        ubuf[...] = jnp.zeros_like(ubuf)

    def shift(up, lo):
        return _shift_rows_roll(ubuf[:, lo:lo + FF_CHUNK], up)

    def store_up(up, lo):
        ubuf[:, lo:lo + FF_CHUNK] = up[t - SUBLANES:]
        fc_ref[0, :, lo:lo + FF_CHUNK] = up[t - SUBLANES:]

    x = x_ref[0]
    h = _ffn_hidden(x, up_ref, cw_ref, cb_ref, shift, store_up)
    y_ref[0] = _ffn_out(x, h, w_down_ref, g_ref, b_ref, alpha)


def _ffn_sample_kernel(x_ref, past_ref, up_ref, cw_ref, cb_ref, w_down_ref, g_ref, b_ref,
                       y_ref, fc_ref, *, alpha):
    bt, t, d = x_ref.shape

    def shift(up, lo):
        u1, u2 = _shift_rows_3d(past_ref[:, :, lo:lo + FF_CHUNK], up.reshape(bt, t, FF_CHUNK))
        return u1.reshape(bt * t, FF_CHUNK), u2.reshape(bt * t, FF_CHUNK)

    def store_up(up, lo):
        fc_ref[:, :, lo:lo + FF_CHUNK] = up.reshape(bt, t, FF_CHUNK)[:, t - (CONV_W - 1):, :]

    x = x_ref[...].reshape(bt * t, d)
    h = _ffn_hidden(x, up_ref, cw_ref, cb_ref, shift, store_up)
    y = _ffn_out(x, h, w_down_ref, g_ref, b_ref, alpha)
    y_ref[...] = y.reshape(bt, t, d)


def _resident(a, layer):
    idx = (layer,) + (0,) * (a.ndim - 1)
    return pl.BlockSpec((None,) + a.shape[1:], lambda *_: idx, pipeline_mode=pl.Buffered(1))


_MIXER_PROMPT_WEIGHTS = ("w_in", "w_gate", "b_gate", "lng", "lnb", "ws", "bst", "mcw",
                         "p_attn", "p_gmlp", "p_conv", "w_o", "ln1g", "ln1b")
_MIXER_SAMPLE_WEIGHTS = ("w_in", "w_gate", "b_gate", "lng", "lnb", "wsbd", "bs8", "mcw",
                         "p_attn", "p_gmlp", "p_conv", "w_o", "ln1g", "ln1b")
_FFN_WEIGHTS = ("w_up", "fcw", "fcb", "w_down", "ln2g", "ln2b")


def _params():
    return pltpu.CompilerParams(dimension_semantics=("arbitrary", "arbitrary"),
                                vmem_limit_bytes=VMEM_LIMIT_BYTES)


def _mixer_prompt(x, w, layer, alpha, n_heads):
    b, s, d = x.shape
    t = PROMPT_TILE
    kv_w = (n_heads // Q_GROUP) * HEAD_DIM
    sc_w = w["mcw"].shape[-1]
    weights = [w[k] for k in _MIXER_PROMPT_WEIGHTS]
    tile = pl.BlockSpec((1, t, d), lambda i, j: (i, j, 0))
    per_seq = lambda rows, width: pl.BlockSpec((1, rows, width), lambda i, j: (i, 0, 0))
    return pl.pallas_call(
        functools.partial(_mixer_prompt_kernel, alpha=alpha, n_heads=n_heads, layer=layer),
        grid=(b, s // t),
        in_specs=[pl.BlockSpec(memory_space=pltpu.SMEM), tile] + [_resident(a, layer) for a in weights],
        out_specs=[tile, per_seq(WINDOW, kv_w), per_seq(WINDOW, kv_w), per_seq(SUBLANES, sc_w)],
        out_shape=[jax.ShapeDtypeStruct((b, s, d), F32),
                   jax.ShapeDtypeStruct((b, WINDOW, kv_w), F32),
                   jax.ShapeDtypeStruct((b, WINDOW, kv_w), F32),
                   jax.ShapeDtypeStruct((b, SUBLANES, sc_w), F32)],
        scratch_shapes=[pltpu.VMEM((WINDOW, kv_w), F32), pltpu.VMEM((WINDOW, kv_w), F32),
                        pltpu.VMEM((SUBLANES, sc_w), F32),
                        pltpu.VMEM((n_heads, WINDOW, 2 * WINDOW), F32)],
        compiler_params=_params(),
        name="mixer_prompt",
    )(w["sinks"], x, *weights)


def _mixer_sample(x, ck, cv, mcs, w, layer, alpha, n_heads):
    b, t, d = x.shape
    bt = SAMPLE_SEQS
    kv_w = ck.shape[-1]
    gm_w = w["lng"].shape[-1]
    sc_w = w["mcw"].shape[-1]
    weights = [w[k] for k in _MIXER_SAMPLE_WEIGHTS]
    seqs = lambda rows, width: pl.BlockSpec((bt, rows, width), lambda i, j: (i, 0, 0))
    state = lambda rows, width: pl.BlockSpec((None, bt, rows, width), lambda i, j: (layer, i, 0, 0))
    return pl.pallas_call(
        functools.partial(_mixer_sample_kernel, alpha=alpha, n_heads=n_heads, layer=layer),
        grid=(b // bt, 1),
        in_specs=[pl.BlockSpec(memory_space=pltpu.SMEM), seqs(t, d), state(WINDOW, kv_w), state(WINDOW, kv_w),
                  state(CONV_W - 1, sc_w)] + [_resident(a, layer) for a in weights],
        out_specs=[seqs(t, d), seqs(WINDOW, kv_w), seqs(WINDOW, kv_w), seqs(CONV_W - 1, sc_w), seqs(t, gm_w)],
        out_shape=[jax.ShapeDtypeStruct((b, t, d), F32),
                   jax.ShapeDtypeStruct((b, WINDOW, kv_w), F32),
                   jax.ShapeDtypeStruct((b, WINDOW, kv_w), F32),
                   jax.ShapeDtypeStruct((b, CONV_W - 1, sc_w), F32),
                   jax.ShapeDtypeStruct((b, t, gm_w), F32)],
        compiler_params=_params(),
        name="mixer_sample",
    )(w["sinks"], x, ck, cv, mcs, *weights)


def _ffn_prompt(x, w, layer, alpha):
    b, s, d = x.shape
    t = FFN_TILE
    d_ff = w["w_down"].shape[1]
    weights = [w[k] for k in _FFN_WEIGHTS]
    tile = pl.BlockSpec((1, t, d), lambda i, j: (i, j, 0))
    return pl.pallas_call(
        functools.partial(_ffn_prompt_kernel, alpha=alpha),
        grid=(b, s // t),
        in_specs=[tile] + [_resident(a, layer) for a in weights],
        out_specs=[tile, pl.BlockSpec((1, SUBLANES, 2 * d_ff), lambda i, j: (i, 0, 0))],
        out_shape=[jax.ShapeDtypeStruct((b, s, d), F32),
                   jax.ShapeDtypeStruct((b, SUBLANES, 2 * d_ff), F32)],
        scratch_shapes=[pltpu.VMEM((SUBLANES, 2 * d_ff), F32)],
        compiler_params=_params(),
        name="ffn_prompt",
    )(x, *weights)


def _ffn_sample(x, past, w, layer, alpha):
    b, t, d = x.shape
    bt = SAMPLE_SEQS
    d_ff = w["w_down"].shape[1]
    weights = [w[k] for k in _FFN_WEIGHTS]
    seqs = lambda rows, width: pl.BlockSpec((bt, rows, width), lambda i, j: (i, 0, 0))
    return pl.pallas_call(
        functools.partial(_ffn_sample_kernel, alpha=alpha),
        grid=(b // bt, 1),
        in_specs=[seqs(t, d), pl.BlockSpec((None, bt, CONV_W - 1, 2 * d_ff), lambda i, j: (layer, i, 0, 0))]
        + [_resident(a, layer) for a in weights],
        out_specs=[seqs(t, d), seqs(CONV_W - 1, 2 * d_ff)],
        out_shape=[jax.ShapeDtypeStruct((b, t, d), F32),
                   jax.ShapeDtypeStruct((b, CONV_W - 1, 2 * d_ff), F32)],
        compiler_params=_params(),
        name="ffn_sample",
    )(x, past, *weights)


def kernel(x_prompt, x_sample, cache_k_win, cache_v_win, state_mixconv, state_ffnconv, w_in, w_gate, b_gate, gmlp_ln_g, gmlp_ln_b, gmlp_ws, gmlp_bs, mixconv_w, attn_sinks, p_attn, p_gmlp, p_conv, w_o, ln1_g, ln1_b, w_up, ffn_conv_w, ffn_conv_b, w_down, ln2_g, ln2_b):
    depth = w_in.shape[0]
    d = x_prompt.shape[-1]
    n_heads = d // 128
    kv_heads = n_heads // Q_GROUP
    alpha = (2.0 * depth) ** 0.25
    dec_b, dec_t = x_sample.shape[:2]
    gm_w = gmlp_ln_g.shape[1]
    gw = gm_w // GM_GROUPS
    rows_s = SAMPLE_SEQS * dec_t

    ws_t = jnp.tril(gmlp_ws[:, :, :dec_t, :dec_t])
    eye = jnp.eye(SAMPLE_SEQS, dtype=F32)
    wsbd = jnp.einsum("ab,lgts->lgatbs", eye, ws_t).reshape(depth, GM_GROUPS, rows_s, rows_s).astype(BF16)
    bs8 = jnp.repeat(jnp.swapaxes(gmlp_bs[:, :, :dec_t], 1, 2), gw, axis=2)

    row = lambda a: a[:, None, :]
    w = dict(
        sinks=attn_sinks,
        w_in=w_in.astype(BF16), w_gate=w_gate.astype(BF16), b_gate=row(b_gate),
        lng=row(gmlp_ln_g), lnb=row(gmlp_ln_b), ws=gmlp_ws, bst=jnp.swapaxes(gmlp_bs, 1, 2),
        wsbd=wsbd, bs8=bs8, mcw=mixconv_w,
        p_attn=p_attn.astype(BF16), p_gmlp=p_gmlp.astype(BF16), p_conv=p_conv.astype(BF16),
        w_o=w_o.astype(BF16), ln1g=row(ln1_g), ln1b=row(ln1_b),
        w_up=w_up.astype(BF16), fcw=ffn_conv_w, fcb=row(ffn_conv_b),
        w_down=w_down.astype(BF16), ln2g=row(ln2_g), ln2b=row(ln2_b))

    xp, xs = x_prompt, x_sample
    ck_all = cache_k_win.reshape(depth, dec_b, WINDOW, kv_heads * HEAD_DIM)
    cv_all = cache_v_win.reshape(depth, dec_b, WINDOW, kv_heads * HEAD_DIM)
    outs = {k: [] for k in ("kp", "vp", "mcp", "fcp", "ks", "vs", "mcs", "fcs", "gvs")}
    for l in range(depth):
        xp, kwin, vwin, mc = _mixer_prompt(xp, w, l, alpha, n_heads)
        xp, fc = _ffn_prompt(xp, w, l, alpha)
        xs, knew, vnew, mcn, gvn = _mixer_sample(xs, ck_all, cv_all, state_mixconv, w, l, alpha, n_heads)
        xs, fcn = _ffn_sample(xs, state_ffnconv, w, l, alpha)
        for key, val in zip(outs, (kwin, vwin, mc, fc, knew, vnew, mcn, fcn, gvn)):
            outs[key].append(val)
    st = {k: jnp.stack(v) for k, v in outs.items()}
    heads = lambda a: a.reshape(a.shape[:3] + (kv_heads, HEAD_DIM))
    last = lambda a: a[:, :, a.shape[2] - (CONV_W - 1):]
    return (xp, xs, heads(st["kp"]), heads(st["vp"]), last(st["mcp"]), last(st["fcp"]),
            heads(st["ks"]), heads(st["vs"]), st["mcs"], st["fcs"], st["gvs"])
```

```python
import functools

import jax
import jax.numpy as jnp
from jax import lax
from jax.experimental import pallas as pl
from jax.experimental.pallas import tpu as pltpu

F32 = jnp.float32
BF16 = jnp.bfloat16

HEAD_DIM = 64
WINDOW = 128
CHUNK = 128
GM_GROUPS = 4
CONV_W = 3
N_BRANCH = 3
LN_EPS = 1e-5
LOG2_E = 1.4426950408889634
Q_GROUP = 4

SUBLANES = 8
LANES = 128
VMEM_LIMIT_BYTES = 60 * 1024 * 1024

PROMPT_TILE = 1024
FFN_TILE = 1024
SAMPLE_SEQS = 32
FF_CHUNK = 256
TAIL_SLABS = 4
SAMPLE_KEYS = 256


def _dot(a, b):
    return jnp.dot(a, b, preferred_element_type=F32)


def _layer_norm(x, g, b):
    mu = jnp.mean(x, axis=-1, keepdims=True)
    xc = x - mu
    var = jnp.mean(xc * xc, axis=-1, keepdims=True)
    return xc * lax.rsqrt(var + LN_EPS) * g + b


def _sigmoid(x):
    return 1.0 / (1.0 + jnp.exp2(x * (-LOG2_E)))


def _softmax_weights(s, sink):
    m = jnp.maximum(jnp.max(s, axis=-1, keepdims=True), sink)
    p = jnp.exp2(s - m)
    denom = jnp.sum(p, axis=-1, keepdims=True) + jnp.exp2(sink - m)
    return p.astype(BF16), denom


def _shift_rows_roll(tail, cur):
    row = lax.broadcasted_iota(jnp.int32, tail.shape, 0)
    outs = []
    for k in (1, 2):
        r = pltpu.roll(cur, k, axis=0)
        head = jnp.where(row < k, pltpu.roll(tail, k, axis=0), r[:SUBLANES])
        outs.append(jnp.concatenate([head, r[SUBLANES:]], axis=0))
    return outs


def _block_diag_causal(ws_g, t, rows):
    r = lax.broadcasted_iota(jnp.int32, (t, t), 0)
    c = lax.broadcasted_iota(jnp.int32, (t, t), 1)
    w = jnp.where(r >= c, ws_g[:t, :t], 0.0).astype(BF16)
    pos = lax.broadcasted_iota(jnp.int32, (rows, t), 0) % t
    expand = jnp.where(pos == lax.broadcasted_iota(jnp.int32, (rows, t), 1), 1.0, 0.0).astype(BF16)
    pos_t = lax.broadcasted_iota(jnp.int32, (t, rows), 1) % t
    expand_t = jnp.where(pos_t == lax.broadcasted_iota(jnp.int32, (t, rows), 0), 1.0, 0.0).astype(BF16)
    tiled = _dot(_dot(expand, w).astype(BF16), expand_t)
    rr = lax.broadcasted_iota(jnp.int32, (rows, rows), 0) // t
    cc = lax.broadcasted_iota(jnp.int32, (rows, rows), 1) // t
    return jnp.where(rr == cc, tiled, 0.0).astype(BF16)


def _shift_rows_3d(past, cur):
    b, rows, c = cur.shape
    n_past = past.shape[1]
    t = lax.broadcasted_iota(jnp.int32, cur.shape, 1)
    tile = jnp.concatenate([past, jnp.zeros((b, rows - n_past, c), cur.dtype)], axis=1)
    s1 = jnp.where(t >= 1, pltpu.roll(cur, 1, axis=1), pltpu.roll(tile, rows - 1, axis=1))
    s2 = jnp.where(t >= 2, pltpu.roll(cur, 2, axis=1), tile)
    return s1, s2


def _project_residual_norm(lhs, w_ref, x, g_ref, b_ref, alpha):
    rows = x.shape[0]
    slab = rows // TAIL_SLABS if rows % (TAIL_SLABS * 2 * SUBLANES) == 0 else rows
    outs = []
    for r in range(0, rows, slab):
        y = alpha * x[r:r + slab] + _dot(lhs[r:r + slab], w_ref[...])
        outs.append(_layer_norm(y, g_ref[...], b_ref[...]))
    return outs[0] if len(outs) == 1 else jnp.concatenate(outs, axis=0)


def _column_slab(refs, lo, width):
    for ref in refs:
        if lo + width <= ref.shape[1]:
            return ref[:, lo:lo + width]
        lo -= ref.shape[1]
    raise ValueError("column range straddles two slabs")


def _merge_and_norm(x, xb, branches, w_gate_refs, b_gate_ref, p_refs, w_o_ref, g_ref, b_ref, alpha):
    d = x.shape[-1]
    merged = None
    for i, (br, p_ref, w_gate_ref) in enumerate(zip(branches, p_refs, w_gate_refs)):
        gate = _sigmoid(_dot(xb, w_gate_ref[...]) + b_gate_ref[:, i * d:(i + 1) * d])
        term = gate * _dot(br.astype(BF16), p_ref[...])
        merged = term if merged is None else merged + term
    return _project_residual_norm(merged.astype(BF16), w_o_ref, x, g_ref, b_ref, alpha)


def _mixer_prompt_kernel(sinks_ref, x_ref, w_in_a_ref, w_in_b_ref, w_gate0_ref, w_gate1_ref, w_gate2_ref, b_gate_ref, lng_ref, lnb_ref,
                         ws_ref, bst_ref, mcw_ref, p_attn_ref, p_gmlp_ref, p_conv_ref, w_o_ref,
                         ln1g_ref, ln1b_ref,
                         y_ref, kwin_ref, vwin_ref, mc_ref,
                         kprev, vprev, ztail, bias_sc, *, alpha, n_heads, layer):
    n = pl.program_id(1)
    t = x_ref.shape[1]
    kv_w = kprev.shape[1]
    q_w = n_heads * HEAD_DIM
    gm_w = lng_ref.shape[1]
    sc_w = mcw_ref.shape[1]

    @pl.when(n == 0)
    def _():
        kprev[...] = jnp.zeros_like(kprev)
        vprev[...] = jnp.zeros_like(vprev)
        ztail[...] = jnp.zeros_like(ztail)

    x = x_ref[0]
    xb = x.astype(BF16)
    off = [0]

    def proj(width):
        lo = off[0]
        off[0] = lo + width
        return _dot(xb, _column_slab((w_in_a_ref, w_in_b_ref), lo, width))

    q = proj(q_w) * (HEAD_DIM ** -0.5 * LOG2_E)
    k = proj(kv_w)
    v = proj(kv_w)
    gu = proj(gm_w)
    gv = proj(gm_w)
    sb = proj(sc_w)
    sc = proj(sc_w)
    sh = proj(sc_w)

    kext = jnp.concatenate([kprev[...], k], axis=0)
    vext = jnp.concatenate([vprev[...], v], axis=0)
    kprev[...] = k[t - WINDOW:]
    vprev[...] = v[t - WINDOW:]
    kwin_ref[0] = k[t - WINDOW:]
    vwin_ref[0] = v[t - WINDOW:]

    ii = lax.broadcasted_iota(jnp.int32, (WINDOW, 2 * WINDOW), 0)
    jj = lax.broadcasted_iota(jnp.int32, (WINDOW, 2 * WINDOW), 1)
    dist = ii + WINDOW - jj
    allowed = (dist >= 0) & (dist <= WINDOW)
    distf = dist.astype(F32)
    for h in range(n_heads):
        slope = 2.0 ** (-8.0 * (h + 1) / n_heads)
        bias_sc[h] = jnp.where(allowed, distf * (-slope * LOG2_E), -jnp.inf)
    first_ok = (jj >= WINDOW) | (n > 0)
    blocks = []
    for i in range(t // WINDOW):
        heads = []
        for kh in range(n_heads // Q_GROUP):
            kk = kext[i * WINDOW:(i + 2) * WINDOW, kh * HEAD_DIM:(kh + 1) * HEAD_DIM].astype(BF16)
            vv = vext[i * WINDOW:(i + 2) * WINDOW, kh * HEAD_DIM:(kh + 1) * HEAD_DIM].astype(BF16)
            hs = range(kh * Q_GROUP, (kh + 1) * Q_GROUP)
            qg = jnp.concatenate([q[i * WINDOW:(i + 1) * WINDOW, h * HEAD_DIM:(h + 1) * HEAD_DIM]
                                  for h in hs], axis=0).astype(BF16)
            s_all = lax.dot_general(qg, kk, (((1,), (1,)), ((), ())), preferred_element_type=F32)
            ps, denoms = [], []
            for g, h in enumerate(hs):
                bias = bias_sc[h]
                if i == 0:
                    bias = jnp.where(first_ok, bias, -jnp.inf)
                p, denom = _softmax_weights(s_all[g * WINDOW:(g + 1) * WINDOW] + bias,
                                            sinks_ref[layer, h] * LOG2_E)
                ps.append(p)
                denoms.append(denom)
            o_all = _dot(jnp.concatenate(ps, axis=0), vv)
            heads += [o_all[g * WINDOW:(g + 1) * WINDOW] / denoms[g] for g in range(Q_GROUP)]
        blocks.append(jnp.concatenate(heads, axis=1))
    attn = jnp.concatenate(blocks, axis=0)

    gvn = _layer_norm(gv, lng_ref[...], lnb_ref[...]).astype(BF16)
    rr = lax.broadcasted_iota(jnp.int32, (CHUNK, CHUNK), 0)
    cc = lax.broadcasted_iota(jnp.int32, (CHUNK, CHUNK), 1)
    gw = gm_w // GM_GROUPS
    cols = []
    for g in range(GM_GROUPS):
        wsg = jnp.where(rr >= cc, ws_ref[g], 0.0).astype(BF16)
        bias = bst_ref[:, g:g + 1]
        rows = [_dot(wsg, gvn[c * CHUNK:(c + 1) * CHUNK, g * gw:(g + 1) * gw]) + bias
                for c in range(t // CHUNK)]
        cols.append(jnp.concatenate(rows, axis=0))
    gm = gu * jnp.concatenate(cols, axis=1)

    z = sc * sh
    z1, z2 = _shift_rows_roll(ztail[...], z)
    ztail[...] = z[t - SUBLANES:]
    mc_ref[0] = z[t - SUBLANES:]
    scv = sb * (mcw_ref[0:1, :] * z2 + mcw_ref[1:2, :] * z1 + mcw_ref[2:3, :] * z)

    y_ref[0] = _merge_and_norm(x, xb, (attn, gm, scv), (w_gate0_ref, w_gate1_ref, w_gate2_ref), b_gate_ref,
                               (p_attn_ref, p_gmlp_ref, p_conv_ref), w_o_ref, ln1g_ref, ln1b_ref, alpha)


def _mixer_sample_kernel(sinks_ref, x_ref, ck_ref, cv_ref, mcs_ref, w_in_a_ref, w_in_b_ref, w_gate0_ref, w_gate1_ref, w_gate2_ref, b_gate_ref,
                         lng_ref, lnb_ref, ws_ref, bs8_ref, mcw_ref, p_attn_ref, p_gmlp_ref,
                         p_conv_ref, w_o_ref, ln1g_ref, ln1b_ref,
                         y_ref, knew_ref, vnew_ref, mc_ref, gvn_ref, wsbd, *, alpha, n_heads, layer):
    bt, t, d = x_ref.shape
    kv_w = ck_ref.shape[2]
    q_w = n_heads * HEAD_DIM
    gm_w = lng_ref.shape[1]
    sc_w = mcw_ref.shape[1]
    rows = bt * t

    x = x_ref[...].reshape(rows, d)
    xb = x.astype(BF16)
    off = [0]

    def proj(width):
        lo = off[0]
        off[0] = lo + width
        return _dot(xb, _column_slab((w_in_a_ref, w_in_b_ref), lo, width))

    q = (proj(q_w) * (HEAD_DIM ** -0.5 * LOG2_E)).reshape(bt, t, q_w)
    k = proj(kv_w).reshape(bt, t, kv_w)
    v = proj(kv_w).reshape(bt, t, kv_w)
    gu = proj(gm_w)
    gv = proj(gm_w)
    sb = proj(sc_w)
    sc = proj(sc_w)
    sh = proj(sc_w)

    ck = ck_ref[...]
    cv = cv_ref[...]
    knew_ref[:, 0:WINDOW - t, :] = ck[:, t:, :]
    knew_ref[:, WINDOW - t:, :] = k
    vnew_ref[:, 0:WINDOW - t, :] = cv[:, t:, :]
    vnew_ref[:, WINDOW - t:, :] = v
    pad = jnp.zeros((bt, SAMPLE_KEYS - WINDOW - t, kv_w), F32)
    kkf = jnp.concatenate([ck, k, pad], axis=1)
    vvf = jnp.concatenate([cv, v, pad], axis=1)

    qrows = Q_GROUP * t
    ri = lax.broadcasted_iota(jnp.int32, (qrows, SAMPLE_KEYS), 0)
    ci = lax.broadcasted_iota(jnp.int32, (qrows, SAMPLE_KEYS), 1)
    dist = ri % t + WINDOW - ci
    valid = (dist >= 0) & (dist <= WINDOW) & (ci < WINDOW + t)
    distf = dist.astype(F32)
    gi = ri // t
    head_outs = []
    for kh in range(n_heads // Q_GROUP):
        kk = kkf[:, :, kh * HEAD_DIM:(kh + 1) * HEAD_DIM].astype(BF16)
        vv = vvf[:, :, kh * HEAD_DIM:(kh + 1) * HEAD_DIM].astype(BF16)
        qg = jnp.concatenate(
            [q[:, :, (kh * Q_GROUP + g) * HEAD_DIM:(kh * Q_GROUP + g + 1) * HEAD_DIM] for g in range(Q_GROUP)],
            axis=1).astype(BF16)
        s = jnp.einsum("bqd,bkd->bqk", qg, kk, preferred_element_type=F32)
        slope = jnp.zeros((qrows, SAMPLE_KEYS), F32)
        sink = jnp.zeros((qrows, 1), F32)
        for g in range(Q_GROUP):
            h = kh * Q_GROUP + g
            slope = jnp.where(gi == g, 2.0 ** (-8.0 * (h + 1) / n_heads) * LOG2_E, slope)
            sink = jnp.where(gi[:, 0:1] == g, sinks_ref[layer, h] * LOG2_E, sink)
        bias = jnp.where(valid, -slope * distf, -jnp.inf)
        p, denom = _softmax_weights(s + bias[None], sink[None])
        o = jnp.einsum("bqk,bkd->bqd", p, vv, preferred_element_type=F32) / denom
        head_outs += [o[:, g * t:(g + 1) * t, :] for g in range(Q_GROUP)]
    attn = jnp.concatenate(head_outs, axis=2).reshape(rows, q_w)

    gvn = _layer_norm(gv, lng_ref[...], lnb_ref[...])
    gvn_ref[...] = gvn.reshape(bt, t, gm_w)
    gvb = gvn.astype(BF16)
    gw = gm_w // GM_GROUPS

    @pl.when(pl.program_id(0) == 0)
    def _():
        for g in range(GM_GROUPS):
            wsbd[g] = _block_diag_causal(ws_ref[g], t, rows)

    sv = jnp.concatenate([_dot(wsbd[g], gvb[:, g * gw:(g + 1) * gw]) for g in range(GM_GROUPS)], axis=1)
    sv = sv.reshape(bt, t, gm_w) + bs8_ref[...][None]
    gm = gu * sv.reshape(rows, gm_w)

    z = (sc * sh).reshape(bt, t, sc_w)
    z1, z2 = _shift_rows_3d(mcs_ref[...], z)
    mc_ref[...] = z[:, t - (CONV_W - 1):, :]
    cz = mcw_ref[0:1, :][None] * z2 + mcw_ref[1:2, :][None] * z1 + mcw_ref[2:3, :][None] * z
    scv = sb * cz.reshape(rows, sc_w)

    y = _merge_and_norm(x, xb, (attn, gm, scv), (w_gate0_ref, w_gate1_ref, w_gate2_ref), b_gate_ref,
                        (p_attn_ref, p_gmlp_ref, p_conv_ref), w_o_ref, ln1g_ref, ln1b_ref, alpha)
    y_ref[...] = y.reshape(bt, t, d)


def _ffn_hidden(x, up_refs, cw_ref, cb_ref, shift, store_up):
    d_ff = up_refs[0].shape[1]
    xb = x.astype(BF16)
    hs = []
    for j in range(d_ff // FF_CHUNK):
        parts = []
        for up_ref, base in zip(up_refs, (0, d_ff)):
            lo = base + j * FF_CHUNK
            up = _dot(xb, up_ref[:, j * FF_CHUNK:(j + 1) * FF_CHUNK])
            u1, u2 = shift(up, lo)
            store_up(up, lo)
            parts.append(cw_ref[0:1, lo:lo + FF_CHUNK] * u2 + cw_ref[1:2, lo:lo + FF_CHUNK] * u1
                         + cw_ref[2:3, lo:lo + FF_CHUNK] * up + cb_ref[:, lo:lo + FF_CHUNK])
        a, g = parts
        hs.append((g * _sigmoid(g) * a).astype(BF16))
    return jnp.concatenate(hs, axis=1)


def _ffn_out(x, h, w_down_ref, g_ref, b_ref, alpha):
    return _project_residual_norm(h, w_down_ref, x, g_ref, b_ref, alpha)


def _ffn_prompt_kernel(x_ref, up_a_ref, up_g_ref, cw_ref, cb_ref, w_down_ref, g_ref, b_ref,
                       y_ref, fc_ref, ubuf, *, alpha):
    t = x_ref.shape[1]

    @pl.when(pl.program_id(1) == 0)
    def _():
        ubuf[...] = jnp.zeros_like(ubuf)

    def shift(up, lo):
        return _shift_rows_roll(ubuf[:, lo:lo + FF_CHUNK], up)

    def store_up(up, lo):
        ubuf[:, lo:lo + FF_CHUNK] = up[t - SUBLANES:]
        fc_ref[0, :, lo:lo + FF_CHUNK] = up[t - SUBLANES:]

    x = x_ref[0]
    h = _ffn_hidden(x, (up_a_ref, up_g_ref), cw_ref, cb_ref, shift, store_up)
    y_ref[0] = _ffn_out(x, h, w_down_ref, g_ref, b_ref, alpha)


def _ffn_sample_kernel(x_ref, past_ref, up_a_ref, up_g_ref, cw_ref, cb_ref, w_down_ref, g_ref, b_ref,
                       y_ref, fc_ref, *, alpha):
    bt, t, d = x_ref.shape

    def shift(up, lo):
        u1, u2 = _shift_rows_3d(past_ref[:, :, lo:lo + FF_CHUNK], up.reshape(bt, t, FF_CHUNK))
        return u1.reshape(bt * t, FF_CHUNK), u2.reshape(bt * t, FF_CHUNK)

    def store_up(up, lo):
        fc_ref[:, :, lo:lo + FF_CHUNK] = up.reshape(bt, t, FF_CHUNK)[:, t - (CONV_W - 1):, :]

    x = x_ref[...].reshape(bt * t, d)
    h = _ffn_hidden(x, (up_a_ref, up_g_ref), cw_ref, cb_ref, shift, store_up)
    y = _ffn_out(x, h, w_down_ref, g_ref, b_ref, alpha)
    y_ref[...] = y.reshape(bt, t, d)


def _resident(a, layer):
    idx = (layer,) + (0,) * (a.ndim - 1)
    return pl.BlockSpec((None,) + a.shape[1:], lambda *_: idx, pipeline_mode=pl.Buffered(1))


_MIXER_PROMPT_WEIGHTS = ("w_in_a", "w_in_b", "w_gate0", "w_gate1", "w_gate2", "b_gate", "lng", "lnb", "ws",
                         "bst", "mcw", "p_attn", "p_gmlp", "p_conv", "w_o", "ln1g", "ln1b")
_MIXER_SAMPLE_WEIGHTS = ("w_in_a", "w_in_b", "w_gate0", "w_gate1", "w_gate2", "b_gate", "lng", "lnb", "ws",
                         "bs8", "mcw", "p_attn", "p_gmlp", "p_conv", "w_o", "ln1g", "ln1b")
_FFN_WEIGHTS = ("w_up_a", "w_up_g", "fcw", "fcb", "w_down", "ln2g", "ln2b")


def _params():
    return pltpu.CompilerParams(dimension_semantics=("arbitrary", "arbitrary"),
                                vmem_limit_bytes=VMEM_LIMIT_BYTES)


def _mixer_prompt(x, w, layer, alpha, n_heads):
    b, s, d = x.shape
    t = PROMPT_TILE
    kv_w = (n_heads // Q_GROUP) * HEAD_DIM
    sc_w = w["mcw"].shape[-1]
    weights = [w[k] for k in _MIXER_PROMPT_WEIGHTS]
    tile = pl.BlockSpec((1, t, d), lambda i, j: (i, j, 0))
    per_seq = lambda rows, width: pl.BlockSpec((1, rows, width), lambda i, j: (i, 0, 0))
    return pl.pallas_call(
        functools.partial(_mixer_prompt_kernel, alpha=alpha, n_heads=n_heads, layer=layer),
        grid=(b, s // t),
        in_specs=[pl.BlockSpec(memory_space=pltpu.SMEM), tile] + [_resident(a, layer) for a in weights],
        out_specs=[tile, per_seq(WINDOW, kv_w), per_seq(WINDOW, kv_w), per_seq(SUBLANES, sc_w)],
        out_shape=[jax.ShapeDtypeStruct((b, s, d), F32),
                   jax.ShapeDtypeStruct((b, WINDOW, kv_w), F32),
                   jax.ShapeDtypeStruct((b, WINDOW, kv_w), F32),
                   jax.ShapeDtypeStruct((b, SUBLANES, sc_w), F32)],
        scratch_shapes=[pltpu.VMEM((WINDOW, kv_w), F32), pltpu.VMEM((WINDOW, kv_w), F32),
                        pltpu.VMEM((SUBLANES, sc_w), F32),
                        pltpu.VMEM((n_heads, WINDOW, 2 * WINDOW), F32)],
        compiler_params=_params(),
        name="mixer_prompt",
    )(w["sinks"], x, *weights)


def _mixer_sample(x, ck, cv, mcs, w, layer, alpha, n_heads):
    b, t, d = x.shape
    bt = SAMPLE_SEQS
    kv_w = ck.shape[-1]
    gm_w = w["lng"].shape[-1]
    sc_w = w["mcw"].shape[-1]
    weights = [w[k] for k in _MIXER_SAMPLE_WEIGHTS]
    seqs = lambda rows, width: pl.BlockSpec((bt, rows, width), lambda i, j: (i, 0, 0))
    state = lambda rows, width: pl.BlockSpec((None, bt, rows, width), lambda i, j: (layer, i, 0, 0))
    return pl.pallas_call(
        functools.partial(_mixer_sample_kernel, alpha=alpha, n_heads=n_heads, layer=layer),
        grid=(b // bt, 1),
        in_specs=[pl.BlockSpec(memory_space=pltpu.SMEM), seqs(t, d), state(WINDOW, kv_w), state(WINDOW, kv_w),
                  state(CONV_W - 1, sc_w)] + [_resident(a, layer) for a in weights],
        out_specs=[seqs(t, d), seqs(WINDOW, kv_w), seqs(WINDOW, kv_w), seqs(CONV_W - 1, sc_w), seqs(t, gm_w)],
        out_shape=[jax.ShapeDtypeStruct((b, t, d), F32),
                   jax.ShapeDtypeStruct((b, WINDOW, kv_w), F32),
                   jax.ShapeDtypeStruct((b, WINDOW, kv_w), F32),
                   jax.ShapeDtypeStruct((b, CONV_W - 1, sc_w), F32),
                   jax.ShapeDtypeStruct((b, t, gm_w), F32)],
        scratch_shapes=[pltpu.VMEM((GM_GROUPS, bt * t, bt * t), BF16)],
        compiler_params=_params(),
        name="mixer_sample",
    )(w["sinks"], x, ck, cv, mcs, *weights)


def _ffn_prompt(x, w, layer, alpha):
    b, s, d = x.shape
    t = FFN_TILE
    d_ff = w["w_down"].shape[1]
    weights = [w[k] for k in _FFN_WEIGHTS]
    tile = pl.BlockSpec((1, t, d), lambda i, j: (i, j, 0))
    return pl.pallas_call(
        functools.partial(_ffn_prompt_kernel, alpha=alpha),
        grid=(b, s // t),
        in_specs=[tile] + [_resident(a, layer) for a in weights],
        out_specs=[tile, pl.BlockSpec((1, SUBLANES, 2 * d_ff), lambda i, j: (i, 0, 0))],
        out_shape=[jax.ShapeDtypeStruct((b, s, d), F32),
                   jax.ShapeDtypeStruct((b, SUBLANES, 2 * d_ff), F32)],
        scratch_shapes=[pltpu.VMEM((SUBLANES, 2 * d_ff), F32)],
        compiler_params=_params(),
        name="ffn_prompt",
    )(x, *weights)


def _ffn_sample(x, past, w, layer, alpha):
    b, t, d = x.shape
    bt = SAMPLE_SEQS
    d_ff = w["w_down"].shape[1]
    weights = [w[k] for k in _FFN_WEIGHTS]
    seqs = lambda rows, width: pl.BlockSpec((bt, rows, width), lambda i, j: (i, 0, 0))
    return pl.pallas_call(
        functools.partial(_ffn_sample_kernel, alpha=alpha),
        grid=(b // bt, 1),
        in_specs=[seqs(t, d), pl.BlockSpec((None, bt, CONV_W - 1, 2 * d_ff), lambda i, j: (layer, i, 0, 0))]
        + [_resident(a, layer) for a in weights],
        out_specs=[seqs(t, d), seqs(CONV_W - 1, 2 * d_ff)],
        out_shape=[jax.ShapeDtypeStruct((b, t, d), F32),
                   jax.ShapeDtypeStruct((b, CONV_W - 1, 2 * d_ff), F32)],
        compiler_params=_params(),
        name="ffn_sample",
    )(x, past, *weights)


def kernel(x_prompt, x_sample, cache_k_win, cache_v_win, state_mixconv, state_ffnconv, w_in, w_gate, b_gate, gmlp_ln_g, gmlp_ln_b, gmlp_ws, gmlp_bs, mixconv_w, attn_sinks, p_attn, p_gmlp, p_conv, w_o, ln1_g, ln1_b, w_up, ffn_conv_w, ffn_conv_b, w_down, ln2_g, ln2_b):
    depth = w_in.shape[0]
    d = x_prompt.shape[-1]
    n_heads = d // 128
    kv_heads = n_heads // Q_GROUP
    alpha = (2.0 * depth) ** 0.25
    dec_b, dec_t = x_sample.shape[:2]
    gm_w = gmlp_ln_g.shape[1]
    gw = gm_w // GM_GROUPS

    bs8 = jnp.repeat(jnp.swapaxes(gmlp_bs[:, :, :dec_t], 1, 2), gw, axis=2)

    row = lambda a: a[:, None, :]
    cols = lambda a, lo, hi: a[:, :, lo:hi].astype(BF16)
    d_ff = w_down.shape[1]
    in_split = n_heads * HEAD_DIM + 2 * kv_heads * HEAD_DIM + 2 * gm_w
    w = dict(
        sinks=attn_sinks,
        w_in_a=cols(w_in, 0, in_split), w_in_b=cols(w_in, in_split, w_in.shape[2]),
        w_gate0=cols(w_gate, 0, d), w_gate1=cols(w_gate, d, 2 * d), w_gate2=cols(w_gate, 2 * d, 3 * d),
        b_gate=row(b_gate),
        lng=row(gmlp_ln_g), lnb=row(gmlp_ln_b), ws=gmlp_ws, bst=jnp.swapaxes(gmlp_bs, 1, 2),
        bs8=bs8, mcw=mixconv_w,
        p_attn=p_attn.astype(BF16), p_gmlp=p_gmlp.astype(BF16), p_conv=p_conv.astype(BF16),
        w_o=w_o.astype(BF16), ln1g=row(ln1_g), ln1b=row(ln1_b),
        w_up_a=cols(w_up, 0, d_ff), w_up_g=cols(w_up, d_ff, 2 * d_ff), fcw=ffn_conv_w, fcb=row(ffn_conv_b),
        w_down=w_down.astype(BF16), ln2g=row(ln2_g), ln2b=row(ln2_b))

    xp, xs = x_prompt, x_sample
    ck_all = cache_k_win.reshape(depth, dec_b, WINDOW, kv_heads * HEAD_DIM)
    cv_all = cache_v_win.reshape(depth, dec_b, WINDOW, kv_heads * HEAD_DIM)
    outs = {k: [] for k in ("kp", "vp", "mcp", "fcp", "ks", "vs", "mcs", "fcs", "gvs")}
    for l in range(depth):
        xp, kwin, vwin, mc = _mixer_prompt(xp, w, l, alpha, n_heads)
        xp, fc = _ffn_prompt(xp, w, l, alpha)
        xs, knew, vnew, mcn, gvn = _mixer_sample(xs, ck_all, cv_all, state_mixconv, w, l, alpha, n_heads)
        xs, fcn = _ffn_sample(xs, state_ffnconv, w, l, alpha)
        for key, val in zip(outs, (kwin, vwin, mc, fc, knew, vnew, mcn, fcn, gvn)):
            outs[key].append(val)
    st = {k: jnp.stack(v) for k, v in outs.items()}
    heads = lambda a: a.reshape(a.shape[:3] + (kv_heads, HEAD_DIM))
    last = lambda a: a[:, :, a.shape[2] - (CONV_W - 1):]
    return (xp, xs, heads(st["kp"]), heads(st["vp"]), last(st["mcp"]), last(st["fcp"]),
            heads(st["ks"]), heads(st["vs"]), st["mcs"], st["fcs"], st["gvs"])
```

```python
import functools

import jax
import jax.numpy as jnp
from jax import lax
from jax.experimental import pallas as pl
from jax.experimental.pallas import tpu as pltpu

F32 = jnp.float32
BF16 = jnp.bfloat16

HEAD_DIM = 64
WINDOW = 128
CHUNK = 128
GM_GROUPS = 4
CONV_W = 3
N_BRANCH = 3
LN_EPS = 1e-5
LOG2_E = 1.4426950408889634
Q_GROUP = 4

SUBLANES = 8
LANES = 128
VMEM_LIMIT_BYTES = 60 * 1024 * 1024

PROMPT_TILE = 1024
FFN_TILE = 1024
SAMPLE_SEQS = 32
FF_CHUNK = 256
TAIL_SLAB_ROWS = 256
SAMPLE_KEYS = 256


def _dot(a, b):
    return jnp.dot(a, b, preferred_element_type=F32)


def _layer_norm(x, g, b):
    mu = jnp.mean(x, axis=-1, keepdims=True)
    xc = x - mu
    var = jnp.mean(xc * xc, axis=-1, keepdims=True)
    return xc * lax.rsqrt(var + LN_EPS) * g + b


def _sigmoid(x):
    return 1.0 / (1.0 + jnp.exp2(x * (-LOG2_E)))


def _softmax_weights(s, sink):
    m = jnp.maximum(jnp.max(s, axis=-1, keepdims=True), sink)
    p = jnp.exp2(s - m)
    denom = jnp.sum(p, axis=-1, keepdims=True) + jnp.exp2(sink - m)
    return p.astype(BF16), denom


def _shift_rows_roll(tail, cur):
    row = lax.broadcasted_iota(jnp.int32, tail.shape, 0)
    outs = []
    for k in (1, 2):
        r = pltpu.roll(cur, k, axis=0)
        head = jnp.where(row < k, pltpu.roll(tail, k, axis=0), r[:SUBLANES])
        outs.append(jnp.concatenate([head, r[SUBLANES:]], axis=0))
    return outs


def _block_diag_causal(ws_g, t, rows):
    r = lax.broadcasted_iota(jnp.int32, (t, t), 0)
    c = lax.broadcasted_iota(jnp.int32, (t, t), 1)
    w = jnp.where(r >= c, ws_g[:t, :t], 0.0).astype(BF16)
    pos = lax.broadcasted_iota(jnp.int32, (rows, t), 0) % t
    expand = jnp.where(pos == lax.broadcasted_iota(jnp.int32, (rows, t), 1), 1.0, 0.0).astype(BF16)
    pos_t = lax.broadcasted_iota(jnp.int32, (t, rows), 1) % t
    expand_t = jnp.where(pos_t == lax.broadcasted_iota(jnp.int32, (t, rows), 0), 1.0, 0.0).astype(BF16)
    tiled = _dot(_dot(expand, w).astype(BF16), expand_t)
    rr = lax.broadcasted_iota(jnp.int32, (rows, rows), 0) // t
    cc = lax.broadcasted_iota(jnp.int32, (rows, rows), 1) // t
    return jnp.where(rr == cc, tiled, 0.0).astype(BF16)


def _shift_rows_3d(past, cur):
    b, rows, c = cur.shape
    n_past = past.shape[1]
    t = lax.broadcasted_iota(jnp.int32, cur.shape, 1)
    tile = jnp.concatenate([past, jnp.zeros((b, rows - n_past, c), cur.dtype)], axis=1)
    s1 = jnp.where(t >= 1, pltpu.roll(cur, 1, axis=1), pltpu.roll(tile, rows - 1, axis=1))
    s2 = jnp.where(t >= 2, pltpu.roll(cur, 2, axis=1), tile)
    return s1, s2


def _project_residual_norm(lhs, w_ref, x, g_ref, b_ref, alpha):
    rows = x.shape[0]
    slab = TAIL_SLAB_ROWS if rows % TAIL_SLAB_ROWS == 0 else rows
    outs = []
    for r in range(0, rows, slab):
        y = alpha * x[r:r + slab] + _dot(lhs[r:r + slab], w_ref[...])
        outs.append(_layer_norm(y, g_ref[...], b_ref[...]))
    return outs[0] if len(outs) == 1 else jnp.concatenate(outs, axis=0)


def _merge_and_norm(x, xb, branches, w_gate_refs, b_gate_ref, p_refs, w_o_ref, g_ref, b_ref, alpha):
    d = x.shape[-1]
    merged = None
    for i, (br, p_ref, w_gate_ref) in enumerate(zip(branches, p_refs, w_gate_refs)):
        gate = _sigmoid(_dot(xb, w_gate_ref[...]) + b_gate_ref[:, i * d:(i + 1) * d])
        term = gate * _dot(br.astype(BF16), p_ref[...])
        merged = term if merged is None else merged + term
    return _project_residual_norm(merged.astype(BF16), w_o_ref, x, g_ref, b_ref, alpha)


def _mixer_prompt_kernel(sinks_ref, x_ref, w_in_ref, w_gate0_ref, w_gate1_ref, w_gate2_ref, b_gate_ref, lng_ref, lnb_ref,
                         ws_ref, bst_ref, mcw_ref, p_attn_ref, p_gmlp_ref, p_conv_ref, w_o_ref,
                         ln1g_ref, ln1b_ref,
                         y_ref, kwin_ref, vwin_ref, mc_ref,
                         kprev, vprev, ztail, bias_sc, *, alpha, n_heads, layer):
    n = pl.program_id(1)
    t = x_ref.shape[1]
    kv_w = kprev.shape[1]
    q_w = n_heads * HEAD_DIM
    gm_w = lng_ref.shape[1]
    sc_w = mcw_ref.shape[1]

    @pl.when(n == 0)
    def _():
        kprev[...] = jnp.zeros_like(kprev)
        vprev[...] = jnp.zeros_like(vprev)
        ztail[...] = jnp.zeros_like(ztail)

    x = x_ref[0]
    xb = x.astype(BF16)
    off = [0]

    def proj(width):
        lo = off[0]
        off[0] = lo + width
        return _dot(xb, w_in_ref[:, lo:lo + width])

    q = proj(q_w) * (HEAD_DIM ** -0.5 * LOG2_E)
    k = proj(kv_w)
    v = proj(kv_w)
    gu = proj(gm_w)
    gv = proj(gm_w)
    sb = proj(sc_w)
    sc = proj(sc_w)
    sh = proj(sc_w)

    kext = jnp.concatenate([kprev[...], k], axis=0)
    vext = jnp.concatenate([vprev[...], v], axis=0)
    kprev[...] = k[t - WINDOW:]
    vprev[...] = v[t - WINDOW:]
    kwin_ref[0] = k[t - WINDOW:]
    vwin_ref[0] = v[t - WINDOW:]

    ii = lax.broadcasted_iota(jnp.int32, (WINDOW, 2 * WINDOW), 0)
    jj = lax.broadcasted_iota(jnp.int32, (WINDOW, 2 * WINDOW), 1)
    dist = ii + WINDOW - jj
    allowed = (dist >= 0) & (dist <= WINDOW)
    distf = dist.astype(F32)
    for h in range(n_heads):
        slope = 2.0 ** (-8.0 * (h + 1) / n_heads)
        bias_sc[h] = jnp.where(allowed, distf * (-slope * LOG2_E), -jnp.inf)
    first_ok = (jj >= WINDOW) | (n > 0)
    blocks = []
    for i in range(t // WINDOW):
        heads = []
        for kh in range(n_heads // Q_GROUP):
            kk = kext[i * WINDOW:(i + 2) * WINDOW, kh * HEAD_DIM:(kh + 1) * HEAD_DIM].astype(BF16)
            vv = vext[i * WINDOW:(i + 2) * WINDOW, kh * HEAD_DIM:(kh + 1) * HEAD_DIM].astype(BF16)
            hs = range(kh * Q_GROUP, (kh + 1) * Q_GROUP)
            qg = jnp.concatenate([q[i * WINDOW:(i + 1) * WINDOW, h * HEAD_DIM:(h + 1) * HEAD_DIM]
                                  for h in hs], axis=0).astype(BF16)
            s_all = lax.dot_general(qg, kk, (((1,), (1,)), ((), ())), preferred_element_type=F32)
            ps, denoms = [], []
            for g, h in enumerate(hs):
                bias = bias_sc[h]
                if i == 0:
                    bias = jnp.where(first_ok, bias, -jnp.inf)
                p, denom = _softmax_weights(s_all[g * WINDOW:(g + 1) * WINDOW] + bias,
                                            sinks_ref[layer, h] * LOG2_E)
                ps.append(p)
                denoms.append(denom)
            o_all = _dot(jnp.concatenate(ps, axis=0), vv)
            heads += [o_all[g * WINDOW:(g + 1) * WINDOW] / denoms[g] for g in range(Q_GROUP)]
        blocks.append(jnp.concatenate(heads, axis=1))
    attn = jnp.concatenate(blocks, axis=0)

    gvn = _layer_norm(gv, lng_ref[...], lnb_ref[...]).astype(BF16)
    rr = lax.broadcasted_iota(jnp.int32, (CHUNK, CHUNK), 0)
    cc = lax.broadcasted_iota(jnp.int32, (CHUNK, CHUNK), 1)
    gw = gm_w // GM_GROUPS
    cols = []
    for g in range(GM_GROUPS):
        wsg = jnp.where(rr >= cc, ws_ref[g], 0.0).astype(BF16)
        bias = bst_ref[:, g:g + 1]
        rows = [_dot(wsg, gvn[c * CHUNK:(c + 1) * CHUNK, g * gw:(g + 1) * gw]) + bias
                for c in range(t // CHUNK)]
        cols.append(jnp.concatenate(rows, axis=0))
    gm = gu * jnp.concatenate(cols, axis=1)

    z = sc * sh
    z1, z2 = _shift_rows_roll(ztail[...], z)
    ztail[...] = z[t - SUBLANES:]
    mc_ref[0] = z[t - SUBLANES:]
    scv = sb * (mcw_ref[0:1, :] * z2 + mcw_ref[1:2, :] * z1 + mcw_ref[2:3, :] * z)

    y_ref[0] = _merge_and_norm(x, xb, (attn, gm, scv), (w_gate0_ref, w_gate1_ref, w_gate2_ref), b_gate_ref,
                               (p_attn_ref, p_gmlp_ref, p_conv_ref), w_o_ref, ln1g_ref, ln1b_ref, alpha)


def _mixer_sample_kernel(sinks_ref, x_ref, ck_ref, cv_ref, mcs_ref, w_in_ref, w_gate0_ref, w_gate1_ref, w_gate2_ref, b_gate_ref,
                         lng_ref, lnb_ref, ws_ref, bs8_ref, mcw_ref, p_attn_ref, p_gmlp_ref,
                         p_conv_ref, w_o_ref, ln1g_ref, ln1b_ref,
                         y_ref, knew_ref, vnew_ref, mc_ref, gvn_ref, wsbd, *, alpha, n_heads, layer):
    bt, t, d = x_ref.shape
    kv_w = ck_ref.shape[2]
    q_w = n_heads * HEAD_DIM
    gm_w = lng_ref.shape[1]
    sc_w = mcw_ref.shape[1]
    rows = bt * t

    x = x_ref[...].reshape(rows, d)
    xb = x.astype(BF16)
    off = [0]

    def proj(width):
        lo = off[0]
        off[0] = lo + width
        return _dot(xb, w_in_ref[:, lo:lo + width])

    q = (proj(q_w) * (HEAD_DIM ** -0.5 * LOG2_E)).reshape(bt, t, q_w)
    k = proj(kv_w).reshape(bt, t, kv_w)
    v = proj(kv_w).reshape(bt, t, kv_w)
    gu = proj(gm_w)
    gv = proj(gm_w)
    sb = proj(sc_w)
    sc = proj(sc_w)
    sh = proj(sc_w)

    ck = ck_ref[...]
    cv = cv_ref[...]
    knew_ref[:, 0:WINDOW - t, :] = ck[:, t:, :]
    knew_ref[:, WINDOW - t:, :] = k
    vnew_ref[:, 0:WINDOW - t, :] = cv[:, t:, :]
    vnew_ref[:, WINDOW - t:, :] = v
    pad = jnp.zeros((bt, SAMPLE_KEYS - WINDOW - t, kv_w), F32)
    kkf = jnp.concatenate([ck, k, pad], axis=1)
    vvf = jnp.concatenate([cv, v, pad], axis=1)

    qrows = Q_GROUP * t
    ri = lax.broadcasted_iota(jnp.int32, (qrows, SAMPLE_KEYS), 0)
    ci = lax.broadcasted_iota(jnp.int32, (qrows, SAMPLE_KEYS), 1)
    dist = ri % t + WINDOW - ci
    valid = (dist >= 0) & (dist <= WINDOW) & (ci < WINDOW + t)
    distf = dist.astype(F32)
    gi = ri // t
    head_outs = []
    for kh in range(n_heads // Q_GROUP):
        kk = kkf[:, :, kh * HEAD_DIM:(kh + 1) * HEAD_DIM].astype(BF16)
        vv = vvf[:, :, kh * HEAD_DIM:(kh + 1) * HEAD_DIM].astype(BF16)
        qg = jnp.concatenate(
            [q[:, :, (kh * Q_GROUP + g) * HEAD_DIM:(kh * Q_GROUP + g + 1) * HEAD_DIM] for g in range(Q_GROUP)],
            axis=1).astype(BF16)
        s = jnp.einsum("bqd,bkd->bqk", qg, kk, preferred_element_type=F32)
        slope = jnp.zeros((qrows, SAMPLE_KEYS), F32)
        sink = jnp.zeros((qrows, 1), F32)
        for g in range(Q_GROUP):
            h = kh * Q_GROUP + g
            slope = jnp.where(gi == g, 2.0 ** (-8.0 * (h + 1) / n_heads) * LOG2_E, slope)
            sink = jnp.where(gi[:, 0:1] == g, sinks_ref[layer, h] * LOG2_E, sink)
        bias = jnp.where(valid, -slope * distf, -jnp.inf)
        p, denom = _softmax_weights(s + bias[None], sink[None])
        o = jnp.einsum("bqk,bkd->bqd", p, vv, preferred_element_type=F32) / denom
        head_outs += [o[:, g * t:(g + 1) * t, :] for g in range(Q_GROUP)]
    attn = jnp.concatenate(head_outs, axis=2).reshape(rows, q_w)

    gvn = _layer_norm(gv, lng_ref[...], lnb_ref[...])
    gvn_ref[...] = gvn.reshape(bt, t, gm_w)
    gvb = gvn.astype(BF16)
    gw = gm_w // GM_GROUPS

    @pl.when(pl.program_id(0) == 0)
    def _():
        for g in range(GM_GROUPS):
            wsbd[g] = _block_diag_causal(ws_ref[g], t, rows)

    sv = jnp.concatenate([_dot(wsbd[g], gvb[:, g * gw:(g + 1) * gw]) for g in range(GM_GROUPS)], axis=1)
    sv = sv.reshape(bt, t, gm_w) + bs8_ref[...][None]
    gm = gu * sv.reshape(rows, gm_w)

    z = (sc * sh).reshape(bt, t, sc_w)
    z1, z2 = _shift_rows_3d(mcs_ref[...], z)
    mc_ref[...] = z[:, t - (CONV_W - 1):, :]
    cz = mcw_ref[0:1, :][None] * z2 + mcw_ref[1:2, :][None] * z1 + mcw_ref[2:3, :][None] * z
    scv = sb * cz.reshape(rows, sc_w)

    y = _merge_and_norm(x, xb, (attn, gm, scv), (w_gate0_ref, w_gate1_ref, w_gate2_ref), b_gate_ref,
                        (p_attn_ref, p_gmlp_ref, p_conv_ref), w_o_ref, ln1g_ref, ln1b_ref, alpha)
    y_ref[...] = y.reshape(bt, t, d)


def _ffn_hidden(x, up_refs, cw_ref, cb_ref, shift, store_up):
    d_ff = up_refs[0].shape[1]
    xb = x.astype(BF16)
    hs = []
    for j in range(d_ff // FF_CHUNK):
        parts = []
        for up_ref, base in zip(up_refs, (0, d_ff)):
            lo = base + j * FF_CHUNK
            up = _dot(xb, up_ref[:, j * FF_CHUNK:(j + 1) * FF_CHUNK])
            u1, u2 = shift(up, lo)
            store_up(up, lo)
            parts.append(cw_ref[0:1, lo:lo + FF_CHUNK] * u2 + cw_ref[1:2, lo:lo + FF_CHUNK] * u1
                         + cw_ref[2:3, lo:lo + FF_CHUNK] * up + cb_ref[:, lo:lo + FF_CHUNK])
        a, g = parts
        hs.append((g * _sigmoid(g) * a).astype(BF16))
    return jnp.concatenate(hs, axis=1)


def _ffn_out(x, h, w_down_ref, g_ref, b_ref, alpha):
    return _project_residual_norm(h, w_down_ref, x, g_ref, b_ref, alpha)


def _ffn_prompt_kernel(x_ref, up_a_ref, up_g_ref, cw_ref, cb_ref, w_down_ref, g_ref, b_ref,
                       y_ref, fc_ref, ubuf, *, alpha):
    t = x_ref.shape[1]

    @pl.when(pl.program_id(1) == 0)
    def _():
        ubuf[...] = jnp.zeros_like(ubuf)

    def shift(up, lo):
        return _shift_rows_roll(ubuf[:, lo:lo + FF_CHUNK], up)

    def store_up(up, lo):
        ubuf[:, lo:lo + FF_CHUNK] = up[t - SUBLANES:]
        fc_ref[0, :, lo:lo + FF_CHUNK] = up[t - SUBLANES:]

    x = x_ref[0]
    h = _ffn_hidden(x, (up_a_ref, up_g_ref), cw_ref, cb_ref, shift, store_up)
    y_ref[0] = _ffn_out(x, h, w_down_ref, g_ref, b_ref, alpha)


def _ffn_sample_kernel(x_ref, past_ref, up_a_ref, up_g_ref, cw_ref, cb_ref, w_down_ref, g_ref, b_ref,
                       y_ref, fc_ref, *, alpha):
    bt, t, d = x_ref.shape

    def shift(up, lo):
        u1, u2 = _shift_rows_3d(past_ref[:, :, lo:lo + FF_CHUNK], up.reshape(bt, t, FF_CHUNK))
        return u1.reshape(bt * t, FF_CHUNK), u2.reshape(bt * t, FF_CHUNK)

    def store_up(up, lo):
        fc_ref[:, :, lo:lo + FF_CHUNK] = up.reshape(bt, t, FF_CHUNK)[:, t - (CONV_W - 1):, :]

    x = x_ref[...].reshape(bt * t, d)
    h = _ffn_hidden(x, (up_a_ref, up_g_ref), cw_ref, cb_ref, shift, store_up)
    y = _ffn_out(x, h, w_down_ref, g_ref, b_ref, alpha)
    y_ref[...] = y.reshape(bt, t, d)


class _ColumnSlab:
    def __init__(self, array, index, count):
        self.array, self.index, self.count = array, index, count


def _operand(a):
    return a.array if isinstance(a, _ColumnSlab) else a


def _resident(a, layer):
    if isinstance(a, _ColumnSlab):
        rows, cols = a.array.shape[1:]
        idx = (layer, 0, a.index)
        return pl.BlockSpec((None, rows, cols // a.count), lambda *_: idx, pipeline_mode=pl.Buffered(1))
    idx = (layer,) + (0,) * (a.ndim - 1)
    return pl.BlockSpec((None,) + a.shape[1:], lambda *_: idx, pipeline_mode=pl.Buffered(1))


_MIXER_PROMPT_WEIGHTS = ("w_in", "w_gate0", "w_gate1", "w_gate2", "b_gate", "lng", "lnb", "ws",
                         "bst", "mcw", "p_attn", "p_gmlp", "p_conv", "w_o", "ln1g", "ln1b")
_MIXER_SAMPLE_WEIGHTS = ("w_in", "w_gate0", "w_gate1", "w_gate2", "b_gate", "lng", "lnb", "ws",
                         "bs8", "mcw", "p_attn", "p_gmlp", "p_conv", "w_o", "ln1g", "ln1b")
_FFN_WEIGHTS = ("w_up_a", "w_up_g", "fcw", "fcb", "w_down", "ln2g", "ln2b")


def _params():
    return pltpu.CompilerParams(dimension_semantics=("arbitrary", "arbitrary"),
                                vmem_limit_bytes=VMEM_LIMIT_BYTES)


def _mixer_prompt(x, w, layer, alpha, n_heads):
    b, s, d = x.shape
    t = PROMPT_TILE
    kv_w = (n_heads // Q_GROUP) * HEAD_DIM
    sc_w = w["mcw"].shape[-1]
    weights = [w[k] for k in _MIXER_PROMPT_WEIGHTS]
    tile = pl.BlockSpec((1, t, d), lambda i, j: (i, j, 0))
    per_seq = lambda rows, width: pl.BlockSpec((1, rows, width), lambda i, j: (i, 0, 0))
    return pl.pallas_call(
        functools.partial(_mixer_prompt_kernel, alpha=alpha, n_heads=n_heads, layer=layer),
        grid=(b, s // t),
        in_specs=[pl.BlockSpec(memory_space=pltpu.SMEM), tile] + [_resident(a, layer) for a in weights],
        out_specs=[tile, per_seq(WINDOW, kv_w), per_seq(WINDOW, kv_w), per_seq(SUBLANES, sc_w)],
        out_shape=[jax.ShapeDtypeStruct((b, s, d), F32),
                   jax.ShapeDtypeStruct((b, WINDOW, kv_w), F32),
                   jax.ShapeDtypeStruct((b, WINDOW, kv_w), F32),
                   jax.ShapeDtypeStruct((b, SUBLANES, sc_w), F32)],
        scratch_shapes=[pltpu.VMEM((WINDOW, kv_w), F32), pltpu.VMEM((WINDOW, kv_w), F32),
                        pltpu.VMEM((SUBLANES, sc_w), F32),
                        pltpu.VMEM((n_heads, WINDOW, 2 * WINDOW), F32)],
        compiler_params=_params(),
        name="mixer_prompt",
    )(w["sinks"], x, *map(_operand, weights))


def _mixer_sample(x, ck, cv, mcs, w, layer, alpha, n_heads):
    b, t, d = x.shape
    bt = SAMPLE_SEQS
    kv_w = ck.shape[-1]
    gm_w = w["lng"].shape[-1]
    sc_w = w["mcw"].shape[-1]
    weights = [w[k] for k in _MIXER_SAMPLE_WEIGHTS]
    seqs = lambda rows, width: pl.BlockSpec((bt, rows, width), lambda i, j: (i, 0, 0))
    state = lambda rows, width: pl.BlockSpec((None, bt, rows, width), lambda i, j: (layer, i, 0, 0))
    return pl.pallas_call(
        functools.partial(_mixer_sample_kernel, alpha=alpha, n_heads=n_heads, layer=layer),
        grid=(b // bt, 1),
        in_specs=[pl.BlockSpec(memory_space=pltpu.SMEM), seqs(t, d), state(WINDOW, kv_w), state(WINDOW, kv_w),
                  state(CONV_W - 1, sc_w)] + [_resident(a, layer) for a in weights],
        out_specs=[seqs(t, d), seqs(WINDOW, kv_w), seqs(WINDOW, kv_w), seqs(CONV_W - 1, sc_w), seqs(t, gm_w)],
        out_shape=[jax.ShapeDtypeStruct((b, t, d), F32),
                   jax.ShapeDtypeStruct((b, WINDOW, kv_w), F32),
                   jax.ShapeDtypeStruct((b, WINDOW, kv_w), F32),
                   jax.ShapeDtypeStruct((b, CONV_W - 1, sc_w), F32),
                   jax.ShapeDtypeStruct((b, t, gm_w), F32)],
        scratch_shapes=[pltpu.VMEM((GM_GROUPS, bt * t, bt * t), BF16)],
        compiler_params=_params(),
        name="mixer_sample",
    )(w["sinks"], x, ck, cv, mcs, *map(_operand, weights))


def _ffn_prompt(x, w, layer, alpha):
    b, s, d = x.shape
    t = FFN_TILE
    d_ff = w["w_down"].shape[1]
    weights = [w[k] for k in _FFN_WEIGHTS]
    tile = pl.BlockSpec((1, t, d), lambda i, j: (i, j, 0))
    return pl.pallas_call(
        functools.partial(_ffn_prompt_kernel, alpha=alpha),
        grid=(b, s // t),
        in_specs=[tile] + [_resident(a, layer) for a in weights],
        out_specs=[tile, pl.BlockSpec((1, SUBLANES, 2 * d_ff), lambda i, j: (i, 0, 0))],
        out_shape=[jax.ShapeDtypeStruct((b, s, d), F32),
                   jax.ShapeDtypeStruct((b, SUBLANES, 2 * d_ff), F32)],
        scratch_shapes=[pltpu.VMEM((SUBLANES, 2 * d_ff), F32)],
        compiler_params=_params(),
        name="ffn_prompt",
    )(x, *map(_operand, weights))


def _ffn_sample(x, past, w, layer, alpha):
    b, t, d = x.shape
    bt = SAMPLE_SEQS
    d_ff = w["w_down"].shape[1]
    weights = [w[k] for k in _FFN_WEIGHTS]
    seqs = lambda rows, width: pl.BlockSpec((bt, rows, width), lambda i, j: (i, 0, 0))
    return pl.pallas_call(
        functools.partial(_ffn_sample_kernel, alpha=alpha),
        grid=(b // bt, 1),
        in_specs=[seqs(t, d), pl.BlockSpec((None, bt, CONV_W - 1, 2 * d_ff), lambda i, j: (layer, i, 0, 0))]
        + [_resident(a, layer) for a in weights],
        out_specs=[seqs(t, d), seqs(CONV_W - 1, 2 * d_ff)],
        out_shape=[jax.ShapeDtypeStruct((b, t, d), F32),
                   jax.ShapeDtypeStruct((b, CONV_W - 1, 2 * d_ff), F32)],
        compiler_params=_params(),
        name="ffn_sample",
    )(x, past, *map(_operand, weights))


def kernel(x_prompt, x_sample, cache_k_win, cache_v_win, state_mixconv, state_ffnconv, w_in, w_gate, b_gate, gmlp_ln_g, gmlp_ln_b, gmlp_ws, gmlp_bs, mixconv_w, attn_sinks, p_attn, p_gmlp, p_conv, w_o, ln1_g, ln1_b, w_up, ffn_conv_w, ffn_conv_b, w_down, ln2_g, ln2_b):
    depth = w_in.shape[0]
    d = x_prompt.shape[-1]
    n_heads = d // 128
    kv_heads = n_heads // Q_GROUP
    alpha = (2.0 * depth) ** 0.25
    dec_b, dec_t = x_sample.shape[:2]
    gm_w = gmlp_ln_g.shape[1]
    gw = gm_w // GM_GROUPS

    bs8 = jnp.repeat(jnp.swapaxes(gmlp_bs[:, :, :dec_t], 1, 2), gw, axis=2)

    row = lambda a: a[:, None, :]
    w_gate_b, w_up_b = w_gate.astype(BF16), w_up.astype(BF16)
    w = dict(
        sinks=attn_sinks,
        w_in=w_in.astype(BF16),
        w_gate0=_ColumnSlab(w_gate_b, 0, N_BRANCH), w_gate1=_ColumnSlab(w_gate_b, 1, N_BRANCH),
        w_gate2=_ColumnSlab(w_gate_b, 2, N_BRANCH), b_gate=row(b_gate),
        lng=row(gmlp_ln_g), lnb=row(gmlp_ln_b), ws=gmlp_ws, bst=jnp.swapaxes(gmlp_bs, 1, 2),
        bs8=bs8, mcw=mixconv_w,
        p_attn=p_attn.astype(BF16), p_gmlp=p_gmlp.astype(BF16), p_conv=p_conv.astype(BF16),
        w_o=w_o.astype(BF16), ln1g=row(ln1_g), ln1b=row(ln1_b),
        w_up_a=_ColumnSlab(w_up_b, 0, 2), w_up_g=_ColumnSlab(w_up_b, 1, 2), fcw=ffn_conv_w, fcb=row(ffn_conv_b),
        w_down=w_down.astype(BF16), ln2g=row(ln2_g), ln2b=row(ln2_b))

    xp, xs = x_prompt, x_sample
    ck_all = cache_k_win.reshape(depth, dec_b, WINDOW, kv_heads * HEAD_DIM)
    cv_all = cache_v_win.reshape(depth, dec_b, WINDOW, kv_heads * HEAD_DIM)
    outs = {k: [] for k in ("kp", "vp", "mcp", "fcp", "ks", "vs", "mcs", "fcs", "gvs")}
    for l in range(depth):
        xp, kwin, vwin, mc = _mixer_prompt(xp, w, l, alpha, n_heads)
        xp, fc = _ffn_prompt(xp, w, l, alpha)
        xs, knew, vnew, mcn, gvn = _mixer_sample(xs, ck_all, cv_all, state_mixconv, w, l, alpha, n_heads)
        xs, fcn = _ffn_sample(xs, state_ffnconv, w, l, alpha)
        for key, val in zip(outs, (kwin, vwin, mc, fc, knew, vnew, mcn, fcn, gvn)):
            outs[key].append(val)
    st = {k: jnp.stack(v) for k, v in outs.items()}
    heads = lambda a: a.reshape(a.shape[:3] + (kv_heads, HEAD_DIM))
    last = lambda a: a[:, :, a.shape[2] - (CONV_W - 1):]
    return (xp, xs, heads(st["kp"]), heads(st["vp"]), last(st["mcp"]), last(st["fcp"]),
            heads(st["ks"]), heads(st["vs"]), st["mcs"], st["fcs"], st["gvs"])
```

```python
import functools

import jax
import jax.numpy as jnp
from jax import lax
from jax.experimental import pallas as pl
from jax.experimental.pallas import tpu as pltpu

F32 = jnp.float32
BF16 = jnp.bfloat16

HEAD_DIM = 64
WINDOW = 128
CHUNK = 128
GM_GROUPS = 4
CONV_W = 3
N_BRANCH = 3
LN_EPS = 1e-5
LOG2_E = 1.4426950408889634
Q_GROUP = 4

SUBLANES = 8
LANES = 128
VMEM_LIMIT_BYTES = 60 * 1024 * 1024

PROMPT_TILE = 1024
FFN_TILE = 1024
SAMPLE_SEQS = 32
FF_CHUNK = 256
TAIL_SLAB_ROWS = 256


def _dot(a, b):
    return jnp.dot(a, b, preferred_element_type=F32)


def _layer_norm(x, g, b):
    mu = jnp.mean(x, axis=-1, keepdims=True)
    xc = x - mu
    var = jnp.mean(xc * xc, axis=-1, keepdims=True)
    return xc * lax.rsqrt(var + LN_EPS) * g + b


def _sigmoid(x):
    return 1.0 / (1.0 + jnp.exp2(x * (-LOG2_E)))


def _softmax_weights(s, sink):
    m = jnp.maximum(jnp.max(s, axis=-1, keepdims=True), sink)
    p = jnp.exp2(s - m)
    denom = jnp.sum(p, axis=-1, keepdims=True) + jnp.exp2(sink - m)
    return p.astype(BF16), denom


def _shift_rows_roll(tail, cur):
    row = lax.broadcasted_iota(jnp.int32, tail.shape, 0)
    outs = []
    for k in (1, 2):
        r = pltpu.roll(cur, k, axis=0)
        head = jnp.where(row < k, pltpu.roll(tail, k, axis=0), r[:SUBLANES])
        outs.append(jnp.concatenate([head, r[SUBLANES:]], axis=0))
    return outs


def _block_diag_causal(ws_g, t, rows):
    r = lax.broadcasted_iota(jnp.int32, (t, t), 0)
    c = lax.broadcasted_iota(jnp.int32, (t, t), 1)
    w = jnp.where(r >= c, ws_g[:t, :t], 0.0).astype(BF16)
    pos = lax.broadcasted_iota(jnp.int32, (rows, t), 0) % t
    expand = jnp.where(pos == lax.broadcasted_iota(jnp.int32, (rows, t), 1), 1.0, 0.0).astype(BF16)
    pos_t = lax.broadcasted_iota(jnp.int32, (t, rows), 1) % t
    expand_t = jnp.where(pos_t == lax.broadcasted_iota(jnp.int32, (t, rows), 0), 1.0, 0.0).astype(BF16)
    tiled = _dot(_dot(expand, w).astype(BF16), expand_t)
    rr = lax.broadcasted_iota(jnp.int32, (rows, rows), 0) // t
    cc = lax.broadcasted_iota(jnp.int32, (rows, rows), 1) // t
    return jnp.where(rr == cc, tiled, 0.0).astype(BF16)


def _shift_rows_3d(past, cur):
    b, rows, c = cur.shape
    n_past = past.shape[1]
    t = lax.broadcasted_iota(jnp.int32, cur.shape, 1)
    tile = jnp.concatenate([past, jnp.zeros((b, rows - n_past, c), cur.dtype)], axis=1)
    s1 = jnp.where(t >= 1, pltpu.roll(cur, 1, axis=1), pltpu.roll(tile, rows - 1, axis=1))
    s2 = jnp.where(t >= 2, pltpu.roll(cur, 2, axis=1), tile)
    return s1, s2


def _project_residual_norm(lhs, w_ref, x, g_ref, b_ref, alpha):
    rows = x.shape[0]
    slab = TAIL_SLAB_ROWS if rows % TAIL_SLAB_ROWS == 0 else rows
    outs = []
    for r in range(0, rows, slab):
        y = alpha * x[r:r + slab] + _dot(lhs[r:r + slab], w_ref[...])
        outs.append(_layer_norm(y, g_ref[...], b_ref[...]))
    return outs[0] if len(outs) == 1 else jnp.concatenate(outs, axis=0)


def _merge_and_norm(x, xb, branches, w_gate_refs, b_gate_ref, p_refs, w_o_ref, g_ref, b_ref, alpha):
    d = x.shape[-1]
    merged = None
    for i, (br, p_ref, w_gate_ref) in enumerate(zip(branches, p_refs, w_gate_refs)):
        gate = _sigmoid(_dot(xb, w_gate_ref[...]) + b_gate_ref[:, i * d:(i + 1) * d])
        term = gate * _dot(br.astype(BF16), p_ref[...])
        merged = term if merged is None else merged + term
    return _project_residual_norm(merged.astype(BF16), w_o_ref, x, g_ref, b_ref, alpha)


def _mixer_prompt_kernel(sinks_ref, x_ref, w_in_ref, w_gate0_ref, w_gate1_ref, w_gate2_ref, b_gate_ref, lng_ref, lnb_ref,
                         ws_ref, bst_ref, mcw_ref, p_attn_ref, p_gmlp_ref, p_conv_ref, w_o_ref,
                         ln1g_ref, ln1b_ref,
                         y_ref, kwin_ref, vwin_ref, mc_ref,
                         kprev, vprev, ztail, bias_sc, *, alpha, n_heads, layer):
    n = pl.program_id(1)
    t = x_ref.shape[1]
    kv_w = kprev.shape[1]
    q_w = n_heads * HEAD_DIM
    gm_w = lng_ref.shape[1]
    sc_w = mcw_ref.shape[1]

    @pl.when(n == 0)
    def _():
        kprev[...] = jnp.zeros_like(kprev)
        vprev[...] = jnp.zeros_like(vprev)
        ztail[...] = jnp.zeros_like(ztail)

    x = x_ref[0]
    xb = x.astype(BF16)
    off = [0]

    def proj(width):
        lo = off[0]
        off[0] = lo + width
        return _dot(xb, w_in_ref[:, lo:lo + width])

    q = proj(q_w) * (HEAD_DIM ** -0.5 * LOG2_E)
    k = proj(kv_w)
    v = proj(kv_w)
    gu = proj(gm_w)
    gv = proj(gm_w)
    sb = proj(sc_w)
    sc = proj(sc_w)
    sh = proj(sc_w)

    kext = jnp.concatenate([kprev[...], k], axis=0)
    vext = jnp.concatenate([vprev[...], v], axis=0)
    kprev[...] = k[t - WINDOW:]
    vprev[...] = v[t - WINDOW:]
    kwin_ref[0] = k[t - WINDOW:]
    vwin_ref[0] = v[t - WINDOW:]

    ii = lax.broadcasted_iota(jnp.int32, (WINDOW, 2 * WINDOW), 0)
    jj = lax.broadcasted_iota(jnp.int32, (WINDOW, 2 * WINDOW), 1)
    dist = ii + WINDOW - jj
    allowed = (dist >= 0) & (dist <= WINDOW)
    distf = dist.astype(F32)
    for h in range(n_heads):
        slope = 2.0 ** (-8.0 * (h + 1) / n_heads)
        bias_sc[h] = jnp.where(allowed, distf * (-slope * LOG2_E), -jnp.inf)
    first_ok = (jj >= WINDOW) | (n > 0)
    blocks = []
    for i in range(t // WINDOW):
        heads = []
        for kh in range(n_heads // Q_GROUP):
            kk = kext[i * WINDOW:(i + 2) * WINDOW, kh * HEAD_DIM:(kh + 1) * HEAD_DIM].astype(BF16)
            vv = vext[i * WINDOW:(i + 2) * WINDOW, kh * HEAD_DIM:(kh + 1) * HEAD_DIM].astype(BF16)
            hs = range(kh * Q_GROUP, (kh + 1) * Q_GROUP)
            qg = jnp.concatenate([q[i * WINDOW:(i + 1) * WINDOW, h * HEAD_DIM:(h + 1) * HEAD_DIM]
                                  for h in hs], axis=0).astype(BF16)
            s_all = lax.dot_general(qg, kk, (((1,), (1,)), ((), ())), preferred_element_type=F32)
            ps, denoms = [], []
            for g, h in enumerate(hs):
                bias = bias_sc[h]
                if i == 0:
                    bias = jnp.where(first_ok, bias, -jnp.inf)
                p, denom = _softmax_weights(s_all[g * WINDOW:(g + 1) * WINDOW] + bias,
                                            sinks_ref[layer, h] * LOG2_E)
                ps.append(p)
                denoms.append(denom)
            o_all = _dot(jnp.concatenate(ps, axis=0), vv)
            heads += [o_all[g * WINDOW:(g + 1) * WINDOW] / denoms[g] for g in range(Q_GROUP)]
        blocks.append(jnp.concatenate(heads, axis=1))
    attn = jnp.concatenate(blocks, axis=0)

    gvn = _layer_norm(gv, lng_ref[...], lnb_ref[...]).astype(BF16)
    rr = lax.broadcasted_iota(jnp.int32, (CHUNK, CHUNK), 0)
    cc = lax.broadcasted_iota(jnp.int32, (CHUNK, CHUNK), 1)
    gw = gm_w // GM_GROUPS
    cols = []
    for g in range(GM_GROUPS):
        wsg = jnp.where(rr >= cc, ws_ref[g], 0.0).astype(BF16)
        bias = bst_ref[:, g:g + 1]
        rows = [_dot(wsg, gvn[c * CHUNK:(c + 1) * CHUNK, g * gw:(g + 1) * gw]) + bias
                for c in range(t // CHUNK)]
        cols.append(jnp.concatenate(rows, axis=0))
    gm = gu * jnp.concatenate(cols, axis=1)

    z = sc * sh
    z1, z2 = _shift_rows_roll(ztail[...], z)
    ztail[...] = z[t - SUBLANES:]
    mc_ref[0] = z[t - SUBLANES:]
    scv = sb * (mcw_ref[0:1, :] * z2 + mcw_ref[1:2, :] * z1 + mcw_ref[2:3, :] * z)

    y_ref[0] = _merge_and_norm(x, xb, (attn, gm, scv), (w_gate0_ref, w_gate1_ref, w_gate2_ref), b_gate_ref,
                               (p_attn_ref, p_gmlp_ref, p_conv_ref), w_o_ref, ln1g_ref, ln1b_ref, alpha)


def _mixer_sample_kernel(sinks_ref, x_ref, ckt_ref, cvt_ref, mcs_ref, wkvt_ref, w_in_ref, w_gate0_ref, w_gate1_ref, w_gate2_ref, b_gate_ref,
                         lng_ref, lnb_ref, ws_ref, bs8_ref, mcw_ref, p_attn_ref, p_gmlp_ref,
                         p_conv_ref, w_o_ref, ln1g_ref, ln1b_ref,
                         y_ref, knewt_ref, vnewt_ref, mc_ref, gvn_ref, wsbd, *, alpha, n_heads, layer):
    bt, t, d = x_ref.shape
    kv_w = ckt_ref.shape[1] * ckt_ref.shape[2]
    q_w = n_heads * HEAD_DIM
    gm_w = lng_ref.shape[1]
    sc_w = mcw_ref.shape[1]
    rows = bt * t

    x = x_ref[...].reshape(rows, d)
    xb = x.astype(BF16)
    off = [0]

    def proj(width):
        lo = off[0]
        off[0] = lo + width
        return _dot(xb, w_in_ref[:, lo:lo + width])

    q = (proj(q_w) * (HEAD_DIM ** -0.5 * LOG2_E)).reshape(bt, t, q_w)
    k = proj(kv_w).reshape(bt, t, kv_w)
    v = proj(kv_w).reshape(bt, t, kv_w)
    gu = proj(gm_w)
    gv = proj(gm_w)
    sb = proj(sc_w)
    sc = proj(sc_w)
    sh = proj(sc_w)

    kv_t = lax.dot_general(wkvt_ref[...], xb, (((1,), (1,)), ((), ())), preferred_element_type=F32)
    qrows = Q_GROUP * t

    def bias_for(n_keys, first_pos, kh):
        ri = lax.broadcasted_iota(jnp.int32, (qrows, n_keys), 0)
        ci = lax.broadcasted_iota(jnp.int32, (qrows, n_keys), 1)
        dist = ri % t + WINDOW - (ci + first_pos)
        valid = (dist >= 0) & (dist <= WINDOW)
        gi = ri // t
        slope = jnp.zeros((qrows, n_keys), F32)
        for g in range(Q_GROUP):
            h = kh * Q_GROUP + g
            slope = jnp.where(gi == g, 2.0 ** (-8.0 * (h + 1) / n_heads) * LOG2_E, slope)
        return jnp.where(valid, -slope * dist.astype(F32), -jnp.inf)

    head_outs = []
    for kh in range(n_heads // Q_GROUP):
        lo, hi = kh * HEAD_DIM, (kh + 1) * HEAD_DIM
        kt = ckt_ref[:, kh]
        vt = cvt_ref[:, kh]
        k_new = k[:, :, lo:hi].astype(BF16)
        v_new = v[:, :, lo:hi].astype(BF16)
        qg = jnp.concatenate(
            [q[:, :, (kh * Q_GROUP + g) * HEAD_DIM:(kh * Q_GROUP + g + 1) * HEAD_DIM] for g in range(Q_GROUP)],
            axis=1).astype(BF16)
        s_old = jnp.einsum("bqd,bdk->bqk", qg, kt.astype(BF16), preferred_element_type=F32)
        s_new = jnp.einsum("bqd,bkd->bqk", qg, k_new, preferred_element_type=F32)
        s_old = s_old + bias_for(WINDOW, 0, kh)[None]
        s_new = s_new + bias_for(t, WINDOW, kh)[None]
        gi = lax.broadcasted_iota(jnp.int32, (qrows, 1), 0) // t
        sink = jnp.zeros((qrows, 1), F32)
        for g in range(Q_GROUP):
            sink = jnp.where(gi == g, sinks_ref[layer, kh * Q_GROUP + g] * LOG2_E, sink)
        sink = sink[None]
        m = jnp.maximum(jnp.maximum(jnp.max(s_old, axis=-1, keepdims=True),
                                    jnp.max(s_new, axis=-1, keepdims=True)), sink)
        p_old = jnp.exp2(s_old - m)
        p_new = jnp.exp2(s_new - m)
        denom = (jnp.sum(p_old, axis=-1, keepdims=True) + jnp.sum(p_new, axis=-1, keepdims=True)
                 + jnp.exp2(sink - m))
        o = (jnp.einsum("bqk,bdk->bqd", p_old.astype(BF16), vt.astype(BF16), preferred_element_type=F32)
             + jnp.einsum("bqk,bkd->bqd", p_new.astype(BF16), v_new, preferred_element_type=F32)) / denom
        head_outs += [o[:, g * t:(g + 1) * t, :] for g in range(Q_GROUP)]
        for b in range(bt):
            knewt_ref[b, kh] = jnp.concatenate([kt[b][:, t:], kv_t[lo:hi, b * t:(b + 1) * t]], axis=1)
            vnewt_ref[b, kh] = jnp.concatenate(
                [vt[b][:, t:], kv_t[kv_w + lo:kv_w + hi, b * t:(b + 1) * t]], axis=1)
    attn = jnp.concatenate(head_outs, axis=2).reshape(rows, q_w)

    gvn = _layer_norm(gv, lng_ref[...], lnb_ref[...])
    gvn_ref[...] = gvn.reshape(bt, t, gm_w)
    gvb = gvn.astype(BF16)
    gw = gm_w // GM_GROUPS

    @pl.when(pl.program_id(0) == 0)
    def _():
        for g in range(GM_GROUPS):
            wsbd[g] = _block_diag_causal(ws_ref[g], t, rows)

    sv = jnp.concatenate([_dot(wsbd[g], gvb[:, g * gw:(g + 1) * gw]) for g in range(GM_GROUPS)], axis=1)
    sv = sv.reshape(bt, t, gm_w) + bs8_ref[...][None]
    gm = gu * sv.reshape(rows, gm_w)

    z = (sc * sh).reshape(bt, t, sc_w)
    z1, z2 = _shift_rows_3d(mcs_ref[...], z)
    mc_ref[...] = z[:, t - (CONV_W - 1):, :]
    cz = mcw_ref[0:1, :][None] * z2 + mcw_ref[1:2, :][None] * z1 + mcw_ref[2:3, :][None] * z
    scv = sb * cz.reshape(rows, sc_w)

    y = _merge_and_norm(x, xb, (attn, gm, scv), (w_gate0_ref, w_gate1_ref, w_gate2_ref), b_gate_ref,
                        (p_attn_ref, p_gmlp_ref, p_conv_ref), w_o_ref, ln1g_ref, ln1b_ref, alpha)
    y_ref[...] = y.reshape(bt, t, d)


def _ffn_hidden(x, up_refs, cw_ref, cb_ref, shift, store_up):
    d_ff = up_refs[0].shape[1]
    xb = x.astype(BF16)
    hs = []
    for j in range(d_ff // FF_CHUNK):
        parts = []
        for up_ref, base in zip(up_refs, (0, d_ff)):
            lo = base + j * FF_CHUNK
            up = _dot(xb, up_ref[:, j * FF_CHUNK:(j + 1) * FF_CHUNK])
            u1, u2 = shift(up, lo)
            store_up(up, lo)
            parts.append(cw_ref[0:1, lo:lo + FF_CHUNK] * u2 + cw_ref[1:2, lo:lo + FF_CHUNK] * u1
                         + cw_ref[2:3, lo:lo + FF_CHUNK] * up + cb_ref[:, lo:lo + FF_CHUNK])
        a, g = parts
        hs.append((g * _sigmoid(g) * a).astype(BF16))
    return jnp.concatenate(hs, axis=1)


def _ffn_out(x, h, w_down_ref, g_ref, b_ref, alpha):
    return _project_residual_norm(h, w_down_ref, x, g_ref, b_ref, alpha)


def _ffn_prompt_kernel(x_ref, up_a_ref, up_g_ref, cw_ref, cb_ref, w_down_ref, g_ref, b_ref,
                       y_ref, fc_ref, ubuf, *, alpha):
    t = x_ref.shape[1]

    @pl.when(pl.program_id(1) == 0)
    def _():
        ubuf[...] = jnp.zeros_like(ubuf)

    def shift(up, lo):
        return _shift_rows_roll(ubuf[:, lo:lo + FF_CHUNK], up)

    def store_up(up, lo):
        ubuf[:, lo:lo + FF_CHUNK] = up[t - SUBLANES:]
        fc_ref[0, :, lo:lo + FF_CHUNK] = up[t - SUBLANES:]

    x = x_ref[0]
    h = _ffn_hidden(x, (up_a_ref, up_g_ref), cw_ref, cb_ref, shift, store_up)
    y_ref[0] = _ffn_out(x, h, w_down_ref, g_ref, b_ref, alpha)


def _ffn_sample_kernel(x_ref, past_ref, up_a_ref, up_g_ref, cw_ref, cb_ref, w_down_ref, g_ref, b_ref,
                       y_ref, fc_ref, *, alpha):
    bt, t, d = x_ref.shape

    def shift(up, lo):
        u1, u2 = _shift_rows_3d(past_ref[:, :, lo:lo + FF_CHUNK], up.reshape(bt, t, FF_CHUNK))
        return u1.reshape(bt * t, FF_CHUNK), u2.reshape(bt * t, FF_CHUNK)

    def store_up(up, lo):
        fc_ref[:, :, lo:lo + FF_CHUNK] = up.reshape(bt, t, FF_CHUNK)[:, t - (CONV_W - 1):, :]

    x = x_ref[...].reshape(bt * t, d)
    h = _ffn_hidden(x, (up_a_ref, up_g_ref), cw_ref, cb_ref, shift, store_up)
    y = _ffn_out(x, h, w_down_ref, g_ref, b_ref, alpha)
    y_ref[...] = y.reshape(bt, t, d)


class _ColumnSlab:
    def __init__(self, array, index, count):
        self.array, self.index, self.count = array, index, count


def _operand(a):
    return a.array if isinstance(a, _ColumnSlab) else a


def _resident(a, layer):
    if isinstance(a, _ColumnSlab):
        rows, cols = a.array.shape[1:]
        idx = (layer, 0, a.index)
        return pl.BlockSpec((None, rows, cols // a.count), lambda *_: idx, pipeline_mode=pl.Buffered(1))
    idx = (layer,) + (0,) * (a.ndim - 1)
    return pl.BlockSpec((None,) + a.shape[1:], lambda *_: idx, pipeline_mode=pl.Buffered(1))


_MIXER_PROMPT_WEIGHTS = ("w_in", "w_gate0", "w_gate1", "w_gate2", "b_gate", "lng", "lnb", "ws",
                         "bst", "mcw", "p_attn", "p_gmlp", "p_conv", "w_o", "ln1g", "ln1b")
_MIXER_SAMPLE_WEIGHTS = ("w_in", "w_gate0", "w_gate1", "w_gate2", "b_gate", "lng", "lnb", "ws",
                         "bs8", "mcw", "p_attn", "p_gmlp", "p_conv", "w_o", "ln1g", "ln1b")
_FFN_WEIGHTS = ("w_up_a", "w_up_g", "fcw", "fcb", "w_down", "ln2g", "ln2b")


def _params():
    return pltpu.CompilerParams(dimension_semantics=("arbitrary", "arbitrary"),
                                vmem_limit_bytes=VMEM_LIMIT_BYTES)


def _mixer_prompt(x, w, layer, alpha, n_heads):
    b, s, d = x.shape
    t = PROMPT_TILE
    kv_w = (n_heads // Q_GROUP) * HEAD_DIM
    sc_w = w["mcw"].shape[-1]
    weights = [w[k] for k in _MIXER_PROMPT_WEIGHTS]
    tile = pl.BlockSpec((1, t, d), lambda i, j: (i, j, 0))
    per_seq = lambda rows, width: pl.BlockSpec((1, rows, width), lambda i, j: (i, 0, 0))
    return pl.pallas_call(
        functools.partial(_mixer_prompt_kernel, alpha=alpha, n_heads=n_heads, layer=layer),
        grid=(b, s // t),
        in_specs=[pl.BlockSpec(memory_space=pltpu.SMEM), tile] + [_resident(a, layer) for a in weights],
        out_specs=[tile, per_seq(WINDOW, kv_w), per_seq(WINDOW, kv_w), per_seq(SUBLANES, sc_w)],
        out_shape=[jax.ShapeDtypeStruct((b, s, d), F32),
                   jax.ShapeDtypeStruct((b, WINDOW, kv_w), F32),
                   jax.ShapeDtypeStruct((b, WINDOW, kv_w), F32),
                   jax.ShapeDtypeStruct((b, SUBLANES, sc_w), F32)],
        scratch_shapes=[pltpu.VMEM((WINDOW, kv_w), F32), pltpu.VMEM((WINDOW, kv_w), F32),
                        pltpu.VMEM((SUBLANES, sc_w), F32),
                        pltpu.VMEM((n_heads, WINDOW, 2 * WINDOW), F32)],
        compiler_params=_params(),
        name="mixer_prompt",
    )(w["sinks"], x, *map(_operand, weights))


def _mixer_sample(x, ckt, cvt, mcs, w, layer, alpha, n_heads):
    b, t, d = x.shape
    bt = SAMPLE_SEQS
    kv_heads = ckt.shape[2]
    gm_w = w["lng"].shape[-1]
    sc_w = w["mcw"].shape[-1]
    weights = [w["wkvt"]] + [w[k] for k in _MIXER_SAMPLE_WEIGHTS]
    seqs = lambda rows, width: pl.BlockSpec((bt, rows, width), lambda i, j: (i, 0, 0))
    state = lambda rows, width: pl.BlockSpec((None, bt, rows, width), lambda i, j: (layer, i, 0, 0))
    window_in = pl.BlockSpec((None, bt, kv_heads, HEAD_DIM, WINDOW), lambda i, j: (layer, i, 0, 0, 0))
    window_out = pl.BlockSpec((bt, kv_heads, HEAD_DIM, WINDOW), lambda i, j: (i, 0, 0, 0))
    return pl.pallas_call(
        functools.partial(_mixer_sample_kernel, alpha=alpha, n_heads=n_heads, layer=layer),
        grid=(b // bt, 1),
        in_specs=[pl.BlockSpec(memory_space=pltpu.SMEM), seqs(t, d), window_in, window_in,
                  state(CONV_W - 1, sc_w)] + [_resident(a, layer) for a in weights],
        out_specs=[seqs(t, d), window_out, window_out, seqs(CONV_W - 1, sc_w), seqs(t, gm_w)],
        out_shape=[jax.ShapeDtypeStruct((b, t, d), F32),
                   jax.ShapeDtypeStruct((b, kv_heads, HEAD_DIM, WINDOW), F32),
                   jax.ShapeDtypeStruct((b, kv_heads, HEAD_DIM, WINDOW), F32),
                   jax.ShapeDtypeStruct((b, CONV_W - 1, sc_w), F32),
                   jax.ShapeDtypeStruct((b, t, gm_w), F32)],
        scratch_shapes=[pltpu.VMEM((GM_GROUPS, bt * t, bt * t), BF16)],
        compiler_params=_params(),
        name="mixer_sample",
    )(w["sinks"], x, ckt, cvt, mcs, *map(_operand, weights))


def _ffn_prompt(x, w, layer, alpha):
    b, s, d = x.shape
    t = FFN_TILE
    d_ff = w["w_down"].shape[1]
    weights = [w[k] for k in _FFN_WEIGHTS]
    tile = pl.BlockSpec((1, t, d), lambda i, j: (i, j, 0))
    return pl.pallas_call(
        functools.partial(_ffn_prompt_kernel, alpha=alpha),
        grid=(b, s // t),
        in_specs=[tile] + [_resident(a, layer) for a in weights],
        out_specs=[tile, pl.BlockSpec((1, SUBLANES, 2 * d_ff), lambda i, j: (i, 0, 0))],
        out_shape=[jax.ShapeDtypeStruct((b, s, d), F32),
                   jax.ShapeDtypeStruct((b, SUBLANES, 2 * d_ff), F32)],
        scratch_shapes=[pltpu.VMEM((SUBLANES, 2 * d_ff), F32)],
        compiler_params=_params(),
        name="ffn_prompt",
    )(x, *map(_operand, weights))


def _ffn_sample(x, past, w, layer, alpha):
    b, t, d = x.shape
    bt = SAMPLE_SEQS
    d_ff = w["w_down"].shape[1]
    weights = [w[k] for k in _FFN_WEIGHTS]
    seqs = lambda rows, width: pl.BlockSpec((bt, rows, width), lambda i, j: (i, 0, 0))
    return pl.pallas_call(
        functools.partial(_ffn_sample_kernel, alpha=alpha),
        grid=(b // bt, 1),
        in_specs=[seqs(t, d), pl.BlockSpec((None, bt, CONV_W - 1, 2 * d_ff), lambda i, j: (layer, i, 0, 0))]
        + [_resident(a, layer) for a in weights],
        out_specs=[seqs(t, d), seqs(CONV_W - 1, 2 * d_ff)],
        out_shape=[jax.ShapeDtypeStruct((b, t, d), F32),
                   jax.ShapeDtypeStruct((b, CONV_W - 1, 2 * d_ff), F32)],
        compiler_params=_params(),
        name="ffn_sample",
    )(x, past, *map(_operand, weights))


def kernel(x_prompt, x_sample, cache_k_win, cache_v_win, state_mixconv, state_ffnconv, w_in, w_gate, b_gate, gmlp_ln_g, gmlp_ln_b, gmlp_ws, gmlp_bs, mixconv_w, attn_sinks, p_attn, p_gmlp, p_conv, w_o, ln1_g, ln1_b, w_up, ffn_conv_w, ffn_conv_b, w_down, ln2_g, ln2_b):
    depth = w_in.shape[0]
    d = x_prompt.shape[-1]
    n_heads = d // 128
    kv_heads = n_heads // Q_GROUP
    alpha = (2.0 * depth) ** 0.25
    dec_b, dec_t = x_sample.shape[:2]
    gm_w = gmlp_ln_g.shape[1]
    gw = gm_w // GM_GROUPS
    q_w, kv_w = n_heads * HEAD_DIM, kv_heads * HEAD_DIM

    bs8 = jnp.repeat(jnp.swapaxes(gmlp_bs[:, :, :dec_t], 1, 2), gw, axis=2)

    row = lambda a: a[:, None, :]
    w_gate_b, w_up_b = w_gate.astype(BF16), w_up.astype(BF16)
    w = dict(
        sinks=attn_sinks,
        w_in=w_in.astype(BF16),
        wkvt=jnp.swapaxes(w_in[:, :, q_w:q_w + 2 * kv_w], 1, 2).astype(BF16),
        w_gate0=_ColumnSlab(w_gate_b, 0, N_BRANCH), w_gate1=_ColumnSlab(w_gate_b, 1, N_BRANCH),
        w_gate2=_ColumnSlab(w_gate_b, 2, N_BRANCH), b_gate=row(b_gate),
        lng=row(gmlp_ln_g), lnb=row(gmlp_ln_b), ws=gmlp_ws, bst=jnp.swapaxes(gmlp_bs, 1, 2),
        bs8=bs8, mcw=mixconv_w,
        p_attn=p_attn.astype(BF16), p_gmlp=p_gmlp.astype(BF16), p_conv=p_conv.astype(BF16),
        w_o=w_o.astype(BF16), ln1g=row(ln1_g), ln1b=row(ln1_b),
        w_up_a=_ColumnSlab(w_up_b, 0, 2), w_up_g=_ColumnSlab(w_up_b, 1, 2), fcw=ffn_conv_w, fcb=row(ffn_conv_b),
        w_down=w_down.astype(BF16), ln2g=row(ln2_g), ln2b=row(ln2_b))

    xp, xs = x_prompt, x_sample
    ck_all = jnp.transpose(cache_k_win, (0, 1, 3, 4, 2))
    cv_all = jnp.transpose(cache_v_win, (0, 1, 3, 4, 2))
    outs = {k: [] for k in ("kp", "vp", "mcp", "fcp", "ks", "vs", "mcs", "fcs", "gvs")}
    for l in range(depth):
        xp, kwin, vwin, mc = _mixer_prompt(xp, w, l, alpha, n_heads)
        xp, fc = _ffn_prompt(xp, w, l, alpha)
        xs, knew, vnew, mcn, gvn = _mixer_sample(xs, ck_all, cv_all, state_mixconv, w, l, alpha, n_heads)
        xs, fcn = _ffn_sample(xs, state_ffnconv, w, l, alpha)
        for key, val in zip(outs, (kwin, vwin, mc, fc, knew, vnew, mcn, fcn, gvn)):
            outs[key].append(val)
    st = {k: jnp.stack(v) for k, v in outs.items()}
    heads = lambda a: a.reshape(a.shape[:3] + (kv_heads, HEAD_DIM))
    last = lambda a: a[:, :, a.shape[2] - (CONV_W - 1):]
    from_t = lambda a: jnp.transpose(a, (0, 1, 4, 2, 3))
    return (xp, xs, heads(st["kp"]), heads(st["vp"]), last(st["mcp"]), last(st["fcp"]),
            from_t(st["ks"]), from_t(st["vs"]), st["mcs"], st["fcs"], st["gvs"])
```

```python
import functools

import jax
import jax.numpy as jnp
from jax import lax
from jax.experimental import pallas as pl
from jax.experimental.pallas import tpu as pltpu

F32 = jnp.float32
BF16 = jnp.bfloat16

HEAD_DIM = 64
WINDOW = 128
CHUNK = 128
GM_GROUPS = 4
CONV_W = 3
N_BRANCH = 3
LN_EPS = 1e-5
LOG2_E = 1.4426950408889634
Q_GROUP = 4

SUBLANES = 8
VMEM_LIMIT_BYTES = 60 * 1024 * 1024

PROMPT_TILE = 1024
FFN_TILE = 1024
SAMPLE_SEQS = 32
FF_CHUNK = 256
TAIL_SLAB_ROWS = 256


def _dot(a, b):
    return jnp.dot(a, b, preferred_element_type=F32)


def _layer_norm(x, g, b):
    mu = jnp.mean(x, axis=-1, keepdims=True)
    xc = x - mu
    var = jnp.mean(xc * xc, axis=-1, keepdims=True)
    return xc * lax.rsqrt(var + LN_EPS) * g + b


def _sigmoid(x):
    return 1.0 / (1.0 + jnp.exp2(x * (-LOG2_E)))


def _softmax_weights(s, sink):
    m = jnp.maximum(jnp.max(s, axis=-1, keepdims=True), sink)
    p = jnp.exp2(s - m)
    denom = jnp.sum(p, axis=-1, keepdims=True) + jnp.exp2(sink - m)
    return p.astype(BF16), denom


def _shift_rows_roll(tail, cur):
    row = lax.broadcasted_iota(jnp.int32, tail.shape, 0)
    outs = []
    for k in (1, 2):
        r = pltpu.roll(cur, k, axis=0)
        head = jnp.where(row < k, pltpu.roll(tail, k, axis=0), r[:SUBLANES])
        outs.append(jnp.concatenate([head, r[SUBLANES:]], axis=0))
    return outs


def _block_diag_causal(ws_g, t, rows):
    r = lax.broadcasted_iota(jnp.int32, (t, t), 0)
    c = lax.broadcasted_iota(jnp.int32, (t, t), 1)
    w = jnp.where(r >= c, ws_g[:t, :t], 0.0).astype(BF16)
    pos = lax.broadcasted_iota(jnp.int32, (rows, t), 0) % t
    expand = jnp.where(pos == lax.broadcasted_iota(jnp.int32, (rows, t), 1), 1.0, 0.0).astype(BF16)
    pos_t = lax.broadcasted_iota(jnp.int32, (t, rows), 1) % t
    expand_t = jnp.where(pos_t == lax.broadcasted_iota(jnp.int32, (t, rows), 0), 1.0, 0.0).astype(BF16)
    tiled = _dot(_dot(expand, w).astype(BF16), expand_t)
    rr = lax.broadcasted_iota(jnp.int32, (rows, rows), 0) // t
    cc = lax.broadcasted_iota(jnp.int32, (rows, rows), 1) // t
    return jnp.where(rr == cc, tiled, 0.0).astype(BF16)


def _shift_rows_3d(past, cur):
    b, rows, c = cur.shape
    n_past = past.shape[1]
    t = lax.broadcasted_iota(jnp.int32, cur.shape, 1)
    tile = jnp.concatenate([past, jnp.zeros((b, rows - n_past, c), cur.dtype)], axis=1)
    s1 = jnp.where(t >= 1, pltpu.roll(cur, 1, axis=1), pltpu.roll(tile, rows - 1, axis=1))
    s2 = jnp.where(t >= 2, pltpu.roll(cur, 2, axis=1), tile)
    return s1, s2


def _project_residual_norm(lhs, w_ref, x, g_ref, b_ref, alpha):
    rows = x.shape[0]
    slab = TAIL_SLAB_ROWS if rows % TAIL_SLAB_ROWS == 0 else rows
    outs = []
    for r in range(0, rows, slab):
        y = alpha * x[r:r + slab] + _dot(lhs[r:r + slab], w_ref[...])
        outs.append(_layer_norm(y, g_ref[...], b_ref[...]))
    return outs[0] if len(outs) == 1 else jnp.concatenate(outs, axis=0)


def _merge_and_norm(x, xb, branches, w_gate_refs, b_gate_ref, p_refs, w_o_ref, g_ref, b_ref, alpha):
    d = x.shape[-1]
    merged = None
    for i, (br, p_ref, w_gate_ref) in enumerate(zip(branches, p_refs, w_gate_refs)):
        gate = _sigmoid(_dot(xb, w_gate_ref[...]) + b_gate_ref[:, i * d:(i + 1) * d])
        term = gate * _dot(br.astype(BF16), p_ref[...])
        merged = term if merged is None else merged + term
    return _project_residual_norm(merged.astype(BF16), w_o_ref, x, g_ref, b_ref, alpha)


def _mixer_prompt_kernel(sinks_ref, x_ref, w_in_ref, w_gate0_ref, w_gate1_ref, w_gate2_ref, b_gate_ref, lng_ref, lnb_ref,
                         ws_ref, bst_ref, mcw_ref, p_attn_ref, p_gmlp_ref, p_conv_ref, w_o_ref,
                         ln1g_ref, ln1b_ref,
                         y_ref, kwin_ref, vwin_ref, mc_ref,
                         kprev, vprev, ztail, bias_sc, *, alpha, n_heads, layer):
    n = pl.program_id(1)
    t = x_ref.shape[1]
    kv_w = kprev.shape[1]
    q_w = n_heads * HEAD_DIM
    gm_w = lng_ref.shape[1]
    sc_w = mcw_ref.shape[1]

    @pl.when(n == 0)
    def _():
        kprev[...] = jnp.zeros_like(kprev)
        vprev[...] = jnp.zeros_like(vprev)
        ztail[...] = jnp.zeros_like(ztail)

    x = x_ref[0]
    xb = x.astype(BF16)
    off = [0]

    def proj(width):
        lo = off[0]
        off[0] = lo + width
        return _dot(xb, w_in_ref[:, lo:lo + width])

    q = proj(q_w) * (HEAD_DIM ** -0.5 * LOG2_E)
    kv = proj(2 * kv_w)
    k, v = kv[:, :kv_w], kv[:, kv_w:]
    gu = proj(gm_w)
    gv = proj(gm_w)
    sb = proj(sc_w)
    sc = proj(sc_w)
    sh = proj(sc_w)

    kext = jnp.concatenate([kprev[...], k], axis=0)
    vext = jnp.concatenate([vprev[...], v], axis=0)
    kprev[...] = k[t - WINDOW:]
    vprev[...] = v[t - WINDOW:]
    kwin_ref[0] = k[t - WINDOW:]
    vwin_ref[0] = v[t - WINDOW:]

    ii = lax.broadcasted_iota(jnp.int32, (WINDOW, 2 * WINDOW), 0)
    jj = lax.broadcasted_iota(jnp.int32, (WINDOW, 2 * WINDOW), 1)
    dist = ii + WINDOW - jj
    allowed = (dist >= 0) & (dist <= WINDOW)
    distf = dist.astype(F32)
    for h in range(n_heads):
        slope = 2.0 ** (-8.0 * (h + 1) / n_heads)
        bias_sc[h] = jnp.where(allowed, distf * (-slope * LOG2_E), -jnp.inf)
    first_ok = (jj >= WINDOW) | (n > 0)
    blocks = []
    for i in range(t // WINDOW):
        heads = []
        for kh in range(n_heads // Q_GROUP):
            kk = kext[i * WINDOW:(i + 2) * WINDOW, kh * HEAD_DIM:(kh + 1) * HEAD_DIM].astype(BF16)
            vv = vext[i * WINDOW:(i + 2) * WINDOW, kh * HEAD_DIM:(kh + 1) * HEAD_DIM].astype(BF16)
            hs = range(kh * Q_GROUP, (kh + 1) * Q_GROUP)
            qg = jnp.concatenate([q[i * WINDOW:(i + 1) * WINDOW, h * HEAD_DIM:(h + 1) * HEAD_DIM]
                                  for h in hs], axis=0).astype(BF16)
            s_all = lax.dot_general(qg, kk, (((1,), (1,)), ((), ())), preferred_element_type=F32)
            ps, denoms = [], []
            for g, h in enumerate(hs):
                bias = bias_sc[h]
                if i == 0:
                    bias = jnp.where(first_ok, bias, -jnp.inf)
                p, denom = _softmax_weights(s_all[g * WINDOW:(g + 1) * WINDOW] + bias,
                                            sinks_ref[layer, h] * LOG2_E)
                ps.append(p)
                denoms.append(denom)
            o_all = _dot(jnp.concatenate(ps, axis=0), vv)
            heads += [o_all[g * WINDOW:(g + 1) * WINDOW] / denoms[g] for g in range(Q_GROUP)]
        blocks.append(jnp.concatenate(heads, axis=1))
    attn = jnp.concatenate(blocks, axis=0)

    gvn = _layer_norm(gv, lng_ref[...], lnb_ref[...]).astype(BF16)
    rr = lax.broadcasted_iota(jnp.int32, (CHUNK, CHUNK), 0)
    cc = lax.broadcasted_iota(jnp.int32, (CHUNK, CHUNK), 1)
    gw = gm_w // GM_GROUPS
    cols = []
    for g in range(GM_GROUPS):
        wsg = jnp.where(rr >= cc, ws_ref[g], 0.0).astype(BF16)
        bias = bst_ref[:, g:g + 1]
        rows = [_dot(wsg, gvn[c * CHUNK:(c + 1) * CHUNK, g * gw:(g + 1) * gw]) + bias
                for c in range(t // CHUNK)]
        cols.append(jnp.concatenate(rows, axis=0))
    gm = gu * jnp.concatenate(cols, axis=1)

    z = sc * sh
    z1, z2 = _shift_rows_roll(ztail[...], z)
    ztail[...] = z[t - SUBLANES:]
    mc_ref[0] = z[t - SUBLANES:]
    scv = sb * (mcw_ref[0:1, :] * z2 + mcw_ref[1:2, :] * z1 + mcw_ref[2:3, :] * z)

    y_ref[0] = _merge_and_norm(x, xb, (attn, gm, scv), (w_gate0_ref, w_gate1_ref, w_gate2_ref), b_gate_ref,
                               (p_attn_ref, p_gmlp_ref, p_conv_ref), w_o_ref, ln1g_ref, ln1b_ref, alpha)


def _mixer_sample_kernel(sinks_ref, x_ref, ckt_ref, cvt_ref, mcs_ref, wkvt_ref, w_in_ref, w_gate0_ref, w_gate1_ref, w_gate2_ref, b_gate_ref,
                         lng_ref, lnb_ref, ws_ref, bs8_ref, mcw_ref, p_attn_ref, p_gmlp_ref,
                         p_conv_ref, w_o_ref, ln1g_ref, ln1b_ref,
                         y_ref, knewt_ref, vnewt_ref, mc_ref, gvn_ref, wsbd, *, alpha, n_heads, layer):
    bt, t, d = x_ref.shape
    kv_w = ckt_ref.shape[1] * ckt_ref.shape[2]
    q_w = n_heads * HEAD_DIM
    gm_w = lng_ref.shape[1]
    sc_w = mcw_ref.shape[1]
    rows = bt * t

    x = x_ref[...].reshape(rows, d)
    xb = x.astype(BF16)
    off = [0]

    def proj(width):
        lo = off[0]
        off[0] = lo + width
        return _dot(xb, w_in_ref[:, lo:lo + width])

    q = (proj(q_w) * (HEAD_DIM ** -0.5 * LOG2_E)).reshape(bt, t, q_w)
    kv = proj(2 * kv_w)
    k = kv[:, :kv_w].reshape(bt, t, kv_w)
    v = kv[:, kv_w:].reshape(bt, t, kv_w)
    gu = proj(gm_w)
    gv = proj(gm_w)
    sb = proj(sc_w)
    sc = proj(sc_w)
    sh = proj(sc_w)

    kv_t = lax.dot_general(wkvt_ref[...], xb, (((1,), (1,)), ((), ())), preferred_element_type=F32)
    qrows = Q_GROUP * t

    def bias_for(n_keys, first_pos, kh):
        ri = lax.broadcasted_iota(jnp.int32, (qrows, n_keys), 0)
        ci = lax.broadcasted_iota(jnp.int32, (qrows, n_keys), 1)
        dist = ri % t + WINDOW - (ci + first_pos)
        valid = (dist >= 0) & (dist <= WINDOW)
        gi = ri // t
        slope = jnp.zeros((qrows, n_keys), F32)
        for g in range(Q_GROUP):
            h = kh * Q_GROUP + g
            slope = jnp.where(gi == g, 2.0 ** (-8.0 * (h + 1) / n_heads) * LOG2_E, slope)
        return jnp.where(valid, -slope * dist.astype(F32), -jnp.inf)

    head_outs = []
    for kh in range(n_heads // Q_GROUP):
        lo, hi = kh * HEAD_DIM, (kh + 1) * HEAD_DIM
        kt = ckt_ref[:, kh]
        vt = cvt_ref[:, kh]
        k_new = k[:, :, lo:hi].astype(BF16)
        v_new = v[:, :, lo:hi].astype(BF16)
        qg = jnp.concatenate(
            [q[:, :, (kh * Q_GROUP + g) * HEAD_DIM:(kh * Q_GROUP + g + 1) * HEAD_DIM] for g in range(Q_GROUP)],
            axis=1).astype(BF16)
        s_old = jnp.einsum("bqd,bdk->bqk", qg, kt.astype(BF16), preferred_element_type=F32)
        s_new = jnp.einsum("bqd,bkd->bqk", qg, k_new, preferred_element_type=F32)
        s_old = s_old + bias_for(WINDOW, 0, kh)[None]
        s_new = s_new + bias_for(t, WINDOW, kh)[None]
        gi = lax.broadcasted_iota(jnp.int32, (qrows, 1), 0) // t
        sink = jnp.zeros((qrows, 1), F32)
        for g in range(Q_GROUP):
            sink = jnp.where(gi == g, sinks_ref[layer, kh * Q_GROUP + g] * LOG2_E, sink)
        sink = sink[None]
        m = jnp.maximum(jnp.maximum(jnp.max(s_old, axis=-1, keepdims=True),
                                    jnp.max(s_new, axis=-1, keepdims=True)), sink)
        p_old = jnp.exp2(s_old - m)
        p_new = jnp.exp2(s_new - m)
        denom = (jnp.sum(p_old, axis=-1, keepdims=True) + jnp.sum(p_new, axis=-1, keepdims=True)
                 + jnp.exp2(sink - m))
        o = (jnp.einsum("bqk,bdk->bqd", p_old.astype(BF16), vt.astype(BF16), preferred_element_type=F32)
             + jnp.einsum("bqk,bkd->bqd", p_new.astype(BF16), v_new, preferred_element_type=F32)) / denom
        head_outs += [o[:, g * t:(g + 1) * t, :] for g in range(Q_GROUP)]
        for b in range(bt):
            knewt_ref[b, kh] = jnp.concatenate([kt[b][:, t:], kv_t[lo:hi, b * t:(b + 1) * t]], axis=1)
            vnewt_ref[b, kh] = jnp.concatenate(
                [vt[b][:, t:], kv_t[kv_w + lo:kv_w + hi, b * t:(b + 1) * t]], axis=1)
    attn = jnp.concatenate(head_outs, axis=2).reshape(rows, q_w)

    gvn = _layer_norm(gv, lng_ref[...], lnb_ref[...])
    gvn_ref[...] = gvn.reshape(bt, t, gm_w)
    gvb = gvn.astype(BF16)
    gw = gm_w // GM_GROUPS

    @pl.when(pl.program_id(0) == 0)
    def _():
        for g in range(GM_GROUPS):
            wsbd[g] = _block_diag_causal(ws_ref[g], t, rows)

    sv = jnp.concatenate([_dot(wsbd[g], gvb[:, g * gw:(g + 1) * gw]) for g in range(GM_GROUPS)], axis=1)
    sv = sv.reshape(bt, t, gm_w) + bs8_ref[...][None]
    gm = gu * sv.reshape(rows, gm_w)

    z = (sc * sh).reshape(bt, t, sc_w)
    z1, z2 = _shift_rows_3d(mcs_ref[...], z)
    mc_ref[...] = z[:, t - (CONV_W - 1):, :]
    cz = mcw_ref[0:1, :][None] * z2 + mcw_ref[1:2, :][None] * z1 + mcw_ref[2:3, :][None] * z
    scv = sb * cz.reshape(rows, sc_w)

    y = _merge_and_norm(x, xb, (attn, gm, scv), (w_gate0_ref, w_gate1_ref, w_gate2_ref), b_gate_ref,
                        (p_attn_ref, p_gmlp_ref, p_conv_ref), w_o_ref, ln1g_ref, ln1b_ref, alpha)
    y_ref[...] = y.reshape(bt, t, d)


def _ffn_hidden(x, up_refs, cw_ref, cb_ref, shift, store_up):
    d_ff = up_refs[0].shape[1]
    xb = x.astype(BF16)
    hs = []
    for j in range(d_ff // FF_CHUNK):
        parts = []
        for up_ref, base in zip(up_refs, (0, d_ff)):
            lo = base + j * FF_CHUNK
            up = _dot(xb, up_ref[:, j * FF_CHUNK:(j + 1) * FF_CHUNK])
            u1, u2 = shift(up, lo)
            store_up(up, lo)
            parts.append(cw_ref[0:1, lo:lo + FF_CHUNK] * u2 + cw_ref[1:2, lo:lo + FF_CHUNK] * u1
                         + cw_ref[2:3, lo:lo + FF_CHUNK] * up + cb_ref[:, lo:lo + FF_CHUNK])
        a, g = parts
        hs.append((g * _sigmoid(g) * a).astype(BF16))
    return jnp.concatenate(hs, axis=1)


def _ffn_out(x, h, w_down_ref, g_ref, b_ref, alpha):
    return _project_residual_norm(h, w_down_ref, x, g_ref, b_ref, alpha)


def _ffn_prompt_kernel(x_ref, up_a_ref, up_g_ref, cw_ref, cb_ref, w_down_ref, g_ref, b_ref,
                       y_ref, fc_ref, ubuf, *, alpha):
    t = x_ref.shape[1]

    @pl.when(pl.program_id(1) == 0)
    def _():
        ubuf[...] = jnp.zeros_like(ubuf)

    def shift(up, lo):
        return _shift_rows_roll(ubuf[:, lo:lo + FF_CHUNK], up)

    def store_up(up, lo):
        ubuf[:, lo:lo + FF_CHUNK] = up[t - SUBLANES:]
        fc_ref[0, :, lo:lo + FF_CHUNK] = up[t - SUBLANES:]

    x = x_ref[0]
    h = _ffn_hidden(x, (up_a_ref, up_g_ref), cw_ref, cb_ref, shift, store_up)
    y_ref[0] = _ffn_out(x, h, w_down_ref, g_ref, b_ref, alpha)


def _ffn_sample_kernel(x_ref, past_ref, up_a_ref, up_g_ref, cw_ref, cb_ref, w_down_ref, g_ref, b_ref,
                       y_ref, fc_ref, *, alpha):
    bt, t, d = x_ref.shape

    def shift(up, lo):
        u1, u2 = _shift_rows_3d(past_ref[:, :, lo:lo + FF_CHUNK], up.reshape(bt, t, FF_CHUNK))
        return u1.reshape(bt * t, FF_CHUNK), u2.reshape(bt * t, FF_CHUNK)

    def store_up(up, lo):
        fc_ref[:, :, lo:lo + FF_CHUNK] = up.reshape(bt, t, FF_CHUNK)[:, t - (CONV_W - 1):, :]

    x = x_ref[...].reshape(bt * t, d)
    h = _ffn_hidden(x, (up_a_ref, up_g_ref), cw_ref, cb_ref, shift, store_up)
    y = _ffn_out(x, h, w_down_ref, g_ref, b_ref, alpha)
    y_ref[...] = y.reshape(bt, t, d)


class _ColumnSlab:
    def __init__(self, array, index, count):
        self.array, self.index, self.count = array, index, count


def _operand(a):
    return a.array if isinstance(a, _ColumnSlab) else a


def _resident(a, layer):
    if isinstance(a, _ColumnSlab):
        rows, cols = a.array.shape[1:]
        idx = (layer, 0, a.index)
        return pl.BlockSpec((None, rows, cols // a.count), lambda *_: idx, pipeline_mode=pl.Buffered(1))
    idx = (layer,) + (0,) * (a.ndim - 1)
    return pl.BlockSpec((None,) + a.shape[1:], lambda *_: idx, pipeline_mode=pl.Buffered(1))


_MIXER_PROMPT_WEIGHTS = ("w_in", "w_gate0", "w_gate1", "w_gate2", "b_gate", "lng", "lnb", "ws",
                         "bst", "mcw", "p_attn", "p_gmlp", "p_conv", "w_o", "ln1g", "ln1b")
_MIXER_SAMPLE_WEIGHTS = ("w_in", "w_gate0", "w_gate1", "w_gate2", "b_gate", "lng", "lnb", "ws",
                         "bs8", "mcw", "p_attn", "p_gmlp", "p_conv", "w_o", "ln1g", "ln1b")
_FFN_WEIGHTS = ("w_up_a", "w_up_g", "fcw", "fcb", "w_down", "ln2g", "ln2b")


def _params():
    return pltpu.CompilerParams(dimension_semantics=("arbitrary", "arbitrary"),
                                vmem_limit_bytes=VMEM_LIMIT_BYTES)


def _mixer_prompt(x, w, layer, alpha, n_heads):
    b, s, d = x.shape
    t = PROMPT_TILE
    kv_w = (n_heads // Q_GROUP) * HEAD_DIM
    sc_w = w["mcw"].shape[-1]
    weights = [w[k] for k in _MIXER_PROMPT_WEIGHTS]
    tile = pl.BlockSpec((1, t, d), lambda i, j: (i, j, 0))
    per_seq = lambda rows, width: pl.BlockSpec((1, rows, width), lambda i, j: (i, 0, 0))
    return pl.pallas_call(
        functools.partial(_mixer_prompt_kernel, alpha=alpha, n_heads=n_heads, layer=layer),
        grid=(b, s // t),
        in_specs=[pl.BlockSpec(memory_space=pltpu.SMEM), tile] + [_resident(a, layer) for a in weights],
        out_specs=[tile, per_seq(WINDOW, kv_w), per_seq(WINDOW, kv_w), per_seq(SUBLANES, sc_w)],
        out_shape=[jax.ShapeDtypeStruct((b, s, d), F32),
                   jax.ShapeDtypeStruct((b, WINDOW, kv_w), F32),
                   jax.ShapeDtypeStruct((b, WINDOW, kv_w), F32),
                   jax.ShapeDtypeStruct((b, SUBLANES, sc_w), F32)],
        scratch_shapes=[pltpu.VMEM((WINDOW, kv_w), F32), pltpu.VMEM((WINDOW, kv_w), F32),
                        pltpu.VMEM((SUBLANES, sc_w), F32),
                        pltpu.VMEM((n_heads, WINDOW, 2 * WINDOW), F32)],
        compiler_params=_params(),
        name="mixer_prompt",
    )(w["sinks"], x, *map(_operand, weights))


def _mixer_sample(x, ckt, cvt, mcs, w, layer, alpha, n_heads):
    b, t, d = x.shape
    bt = SAMPLE_SEQS
    kv_heads = ckt.shape[2]
    gm_w = w["lng"].shape[-1]
    sc_w = w["mcw"].shape[-1]
    weights = [w["wkvt"]] + [w[k] for k in _MIXER_SAMPLE_WEIGHTS]
    seqs = lambda rows, width: pl.BlockSpec((bt, rows, width), lambda i, j: (i, 0, 0))
    state = lambda rows, width: pl.BlockSpec((None, bt, rows, width), lambda i, j: (layer, i, 0, 0))
    window_in = pl.BlockSpec((None, bt, kv_heads, HEAD_DIM, WINDOW), lambda i, j: (layer, i, 0, 0, 0))
    window_out = pl.BlockSpec((bt, kv_heads, HEAD_DIM, WINDOW), lambda i, j: (i, 0, 0, 0))
    return pl.pallas_call(
        functools.partial(_mixer_sample_kernel, alpha=alpha, n_heads=n_heads, layer=layer),
        grid=(b // bt, 1),
        in_specs=[pl.BlockSpec(memory_space=pltpu.SMEM), seqs(t, d), window_in, window_in,
                  state(CONV_W - 1, sc_w)] + [_resident(a, layer) for a in weights],
        out_specs=[seqs(t, d), window_out, window_out, seqs(CONV_W - 1, sc_w), seqs(t, gm_w)],
        out_shape=[jax.ShapeDtypeStruct((b, t, d), F32),
                   jax.ShapeDtypeStruct((b, kv_heads, HEAD_DIM, WINDOW), F32),
                   jax.ShapeDtypeStruct((b, kv_heads, HEAD_DIM, WINDOW), F32),
                   jax.ShapeDtypeStruct((b, CONV_W - 1, sc_w), F32),
                   jax.ShapeDtypeStruct((b, t, gm_w), F32)],
        scratch_shapes=[pltpu.VMEM((GM_GROUPS, bt * t, bt * t), BF16)],
        compiler_params=_params(),
        name="mixer_sample",
    )(w["sinks"], x, ckt, cvt, mcs, *map(_operand, weights))


def _ffn_prompt(x, w, layer, alpha):
    b, s, d = x.shape
    t = FFN_TILE
    d_ff = w["w_down"].shape[1]
    weights = [w[k] for k in _FFN_WEIGHTS]
    tile = pl.BlockSpec((1, t, d), lambda i, j: (i, j, 0))
    return pl.pallas_call(
        functools.partial(_ffn_prompt_kernel, alpha=alpha),
        grid=(b, s // t),
        in_specs=[tile] + [_resident(a, layer) for a in weights],
        out_specs=[tile, pl.BlockSpec((1, SUBLANES, 2 * d_ff), lambda i, j: (i, 0, 0))],
        out_shape=[jax.ShapeDtypeStruct((b, s, d), F32),
                   jax.ShapeDtypeStruct((b, SUBLANES, 2 * d_ff), F32)],
        scratch_shapes=[pltpu.VMEM((SUBLANES, 2 * d_ff), F32)],
        compiler_params=_params(),
        name="ffn_prompt",
    )(x, *map(_operand, weights))


def _ffn_sample(x, past, w, layer, alpha):
    b, t, d = x.shape
    bt = SAMPLE_SEQS
    d_ff = w["w_down"].shape[1]
    weights = [w[k] for k in _FFN_WEIGHTS]
    seqs = lambda rows, width: pl.BlockSpec((bt, rows, width), lambda i, j: (i, 0, 0))
    return pl.pallas_call(
        functools.partial(_ffn_sample_kernel, alpha=alpha),
        grid=(b // bt, 1),
        in_specs=[seqs(t, d), pl.BlockSpec((None, bt, CONV_W - 1, 2 * d_ff), lambda i, j: (layer, i, 0, 0))]
        + [_resident(a, layer) for a in weights],
        out_specs=[seqs(t, d), seqs(CONV_W - 1, 2 * d_ff)],
        out_shape=[jax.ShapeDtypeStruct((b, t, d), F32),
                   jax.ShapeDtypeStruct((b, CONV_W - 1, 2 * d_ff), F32)],
        compiler_params=_params(),
        name="ffn_sample",
    )(x, past, *map(_operand, weights))


def kernel(x_prompt, x_sample, cache_k_win, cache_v_win, state_mixconv, state_ffnconv, w_in, w_gate, b_gate, gmlp_ln_g, gmlp_ln_b, gmlp_ws, gmlp_bs, mixconv_w, attn_sinks, p_attn, p_gmlp, p_conv, w_o, ln1_g, ln1_b, w_up, ffn_conv_w, ffn_conv_b, w_down, ln2_g, ln2_b):
    depth = w_in.shape[0]
    d = x_prompt.shape[-1]
    n_heads = d // 128
    kv_heads = n_heads // Q_GROUP
    alpha = (2.0 * depth) ** 0.25
    dec_b, dec_t = x_sample.shape[:2]
    gm_w = gmlp_ln_g.shape[1]
    gw = gm_w // GM_GROUPS
    q_w, kv_w = n_heads * HEAD_DIM, kv_heads * HEAD_DIM

    bs8 = jnp.repeat(jnp.swapaxes(gmlp_bs[:, :, :dec_t], 1, 2), gw, axis=2)

    row = lambda a: a[:, None, :]
    w_gate_b, w_up_b = w_gate.astype(BF16), w_up.astype(BF16)
    w = dict(
        sinks=attn_sinks,
        w_in=w_in.astype(BF16),
        wkvt=jnp.swapaxes(w_in[:, :, q_w:q_w + 2 * kv_w], 1, 2).astype(BF16),
        w_gate0=_ColumnSlab(w_gate_b, 0, N_BRANCH), w_gate1=_ColumnSlab(w_gate_b, 1, N_BRANCH),
        w_gate2=_ColumnSlab(w_gate_b, 2, N_BRANCH), b_gate=row(b_gate),
        lng=row(gmlp_ln_g), lnb=row(gmlp_ln_b), ws=gmlp_ws, bst=jnp.swapaxes(gmlp_bs, 1, 2),
        bs8=bs8, mcw=mixconv_w,
        p_attn=p_attn.astype(BF16), p_gmlp=p_gmlp.astype(BF16), p_conv=p_conv.astype(BF16),
        w_o=w_o.astype(BF16), ln1g=row(ln1_g), ln1b=row(ln1_b),
        w_up_a=_ColumnSlab(w_up_b, 0, 2), w_up_g=_ColumnSlab(w_up_b, 1, 2), fcw=ffn_conv_w, fcb=row(ffn_conv_b),
        w_down=w_down.astype(BF16), ln2g=row(ln2_g), ln2b=row(ln2_b))

    xp, xs = x_prompt, x_sample
    ck_all = jnp.transpose(cache_k_win, (0, 1, 3, 4, 2))
    cv_all = jnp.transpose(cache_v_win, (0, 1, 3, 4, 2))
    outs = {k: [] for k in ("kp", "vp", "mcp", "fcp", "ks", "vs", "mcs", "fcs", "gvs")}
    for l in range(depth):
        xp, kwin, vwin, mc = _mixer_prompt(xp, w, l, alpha, n_heads)
        xp, fc = _ffn_prompt(xp, w, l, alpha)
        xs, knew, vnew, mcn, gvn = _mixer_sample(xs, ck_all, cv_all, state_mixconv, w, l, alpha, n_heads)
        xs, fcn = _ffn_sample(xs, state_ffnconv, w, l, alpha)
        for key, val in zip(outs, (kwin, vwin, mc, fc, knew, vnew, mcn, fcn, gvn)):
            outs[key].append(val)
    st = {k: jnp.stack(v) for k, v in outs.items()}
    heads = lambda a: a.reshape(a.shape[:3] + (kv_heads, HEAD_DIM))
    last = lambda a: a[:, :, a.shape[2] - (CONV_W - 1):]
    from_t = lambda a: jnp.transpose(a, (0, 1, 4, 2, 3))
    return (xp, xs, heads(st["kp"]), heads(st["vp"]), last(st["mcp"]), last(st["fcp"]),
            from_t(st["ks"]), from_t(st["vs"]), st["mcs"], st["fcs"], st["gvs"])
```

```python
import functools

import jax
import jax.numpy as jnp
from jax import lax
from jax.experimental import pallas as pl
from jax.experimental.pallas import tpu as pltpu

F32 = jnp.float32
BF16 = jnp.bfloat16

HEAD_DIM = 64
WINDOW = 128
CHUNK = 128
GM_GROUPS = 4
CONV_W = 3
N_BRANCH = 3
LN_EPS = 1e-5
LOG2_E = 1.4426950408889634
Q_GROUP = 4

SUBLANES = 8
VMEM_LIMIT_BYTES = 60 * 1024 * 1024

PROMPT_TILE = 1024
FFN_TILE = 1024
SAMPLE_SEQS = 32
FF_CHUNK = 256
TAIL_SLAB_ROWS = 256


def _dot(a, b):
    return jnp.dot(a, b, preferred_element_type=F32)


def _layer_norm(x, g, b):
    mu = jnp.mean(x, axis=-1, keepdims=True)
    xc = x - mu
    var = jnp.mean(xc * xc, axis=-1, keepdims=True)
    return xc * lax.rsqrt(var + LN_EPS) * g + b


def _sigmoid(x):
    return 1.0 / (1.0 + jnp.exp2(x * (-LOG2_E)))


def _softmax_weights(s, sink):
    m = jnp.maximum(jnp.max(s, axis=-1, keepdims=True), sink)
    p = jnp.exp2(s - m)
    denom = jnp.sum(p, axis=-1, keepdims=True) + jnp.exp2(sink - m)
    return p.astype(BF16), denom


def _shift_rows_roll(tail, cur):
    row = lax.broadcasted_iota(jnp.int32, tail.shape, 0)
    outs = []
    for k in (1, 2):
        r = pltpu.roll(cur, k, axis=0)
        head = jnp.where(row < k, pltpu.roll(tail, k, axis=0), r[:SUBLANES])
        outs.append(jnp.concatenate([head, r[SUBLANES:]], axis=0))
    return outs


def _block_diag_causal(ws_g, t, rows):
    r = lax.broadcasted_iota(jnp.int32, (t, t), 0)
    c = lax.broadcasted_iota(jnp.int32, (t, t), 1)
    w = jnp.where(r >= c, ws_g[:t, :t], 0.0).astype(BF16)
    pos = lax.broadcasted_iota(jnp.int32, (rows, t), 0) % t
    expand = jnp.where(pos == lax.broadcasted_iota(jnp.int32, (rows, t), 1), 1.0, 0.0).astype(BF16)
    pos_t = lax.broadcasted_iota(jnp.int32, (t, rows), 1) % t
    expand_t = jnp.where(pos_t == lax.broadcasted_iota(jnp.int32, (t, rows), 0), 1.0, 0.0).astype(BF16)
    tiled = _dot(_dot(expand, w).astype(BF16), expand_t)
    rr = lax.broadcasted_iota(jnp.int32, (rows, rows), 0) // t
    cc = lax.broadcasted_iota(jnp.int32, (rows, rows), 1) // t
    return jnp.where(rr == cc, tiled, 0.0).astype(BF16)


def _shift_rows_3d(past, cur):
    b, rows, c = cur.shape
    n_past = past.shape[1]
    t = lax.broadcasted_iota(jnp.int32, cur.shape, 1)
    tile = jnp.concatenate([past, jnp.zeros((b, rows - n_past, c), cur.dtype)], axis=1)
    s1 = jnp.where(t >= 1, pltpu.roll(cur, 1, axis=1), pltpu.roll(tile, rows - 1, axis=1))
    s2 = jnp.where(t >= 2, pltpu.roll(cur, 2, axis=1), tile)
    return s1, s2


def _project_residual_norm(lhs, w_ref, x, g_ref, b_ref, alpha):
    rows = x.shape[0]
    slab = TAIL_SLAB_ROWS if rows % TAIL_SLAB_ROWS == 0 else rows
    outs = []
    for r in range(0, rows, slab):
        y = alpha * x[r:r + slab] + _dot(lhs[r:r + slab], w_ref[...])
        outs.append(_layer_norm(y, g_ref[...], b_ref[...]))
    return outs[0] if len(outs) == 1 else jnp.concatenate(outs, axis=0)


def _merge_and_norm(x, xb, branches, w_gate_refs, b_gate_ref, p_refs, w_o_ref, g_ref, b_ref, alpha):
    d = x.shape[-1]
    merged = None
    for i, (br, p_ref, w_gate_ref) in enumerate(zip(branches, p_refs, w_gate_refs)):
        gate = _sigmoid(_dot(xb, w_gate_ref[...]) + b_gate_ref[:, i * d:(i + 1) * d])
        term = gate * _dot(br.astype(BF16), p_ref[...])
        merged = term if merged is None else merged + term
    return _project_residual_norm(merged.astype(BF16), w_o_ref, x, g_ref, b_ref, alpha)


def _mixer_prompt_kernel(sinks_ref, x_ref, w_in_ref, w_gate0_ref, w_gate1_ref, w_gate2_ref, b_gate_ref,
                         lng_ref, lnb_ref,
                         ws_ref, bst_ref, mcw_ref, p_attn_ref, p_gmlp_ref, p_conv_ref, w_o_ref,
                         ln1g_ref, ln1b_ref,
                         y_ref, kwin_ref, vwin_ref, mc_ref,
                         kprev, vprev, ztail, bias_sc, *, alpha, n_heads, layer):
    n = pl.program_id(1)
    t = x_ref.shape[1]
    kv_w = kprev.shape[1]
    q_w = n_heads * HEAD_DIM
    gm_w = lng_ref.shape[1]
    sc_w = mcw_ref.shape[1]

    @pl.when(n == 0)
    def _():
        kprev[...] = jnp.zeros_like(kprev)
        vprev[...] = jnp.zeros_like(vprev)
        ztail[...] = jnp.zeros_like(ztail)

    x = x_ref[0]
    xb = x.astype(BF16)
    off = [0]

    def proj(width):
        lo = off[0]
        off[0] = lo + width
        return _dot(xb, w_in_ref[:, lo:lo + width])

    q = proj(q_w) * (HEAD_DIM ** -0.5 * LOG2_E)
    kv = proj(2 * kv_w)
    k, v = kv[:, :kv_w], kv[:, kv_w:]
    gu = proj(gm_w)
    gv = proj(gm_w)
    sb = proj(sc_w)
    sc = proj(sc_w)
    sh = proj(sc_w)

    kext = jnp.concatenate([kprev[...], k], axis=0)
    vext = jnp.concatenate([vprev[...], v], axis=0)
    kprev[...] = k[t - WINDOW:]
    vprev[...] = v[t - WINDOW:]
    kwin_ref[0] = k[t - WINDOW:]
    vwin_ref[0] = v[t - WINDOW:]

    ii = lax.broadcasted_iota(jnp.int32, (WINDOW, 2 * WINDOW), 0)
    jj = lax.broadcasted_iota(jnp.int32, (WINDOW, 2 * WINDOW), 1)
    dist = ii + WINDOW - jj
    allowed = (dist >= 0) & (dist <= WINDOW)
    distf = dist.astype(F32)
    for h in range(n_heads):
        slope = 2.0 ** (-8.0 * (h + 1) / n_heads)
        bias_sc[h] = jnp.where(allowed, distf * (-slope * LOG2_E), -jnp.inf)
    first_ok = (jj >= WINDOW) | (n > 0)
    blocks = []
    for i in range(t // WINDOW):
        heads = []
        for kh in range(n_heads // Q_GROUP):
            kk = kext[i * WINDOW:(i + 2) * WINDOW, kh * HEAD_DIM:(kh + 1) * HEAD_DIM].astype(BF16)
            vv = vext[i * WINDOW:(i + 2) * WINDOW, kh * HEAD_DIM:(kh + 1) * HEAD_DIM].astype(BF16)
            hs = range(kh * Q_GROUP, (kh + 1) * Q_GROUP)
            qg = jnp.concatenate([q[i * WINDOW:(i + 1) * WINDOW, h * HEAD_DIM:(h + 1) * HEAD_DIM]
                                  for h in hs], axis=0).astype(BF16)
            s_all = lax.dot_general(qg, kk, (((1,), (1,)), ((), ())), preferred_element_type=F32)
            ps, denoms = [], []
            for g, h in enumerate(hs):
                bias = bias_sc[h]
                if i == 0:
                    bias = jnp.where(first_ok, bias, -jnp.inf)
                p, denom = _softmax_weights(s_all[g * WINDOW:(g + 1) * WINDOW] + bias,
                                            sinks_ref[layer, h] * LOG2_E)
                ps.append(p)
                denoms.append(denom)
            o_all = _dot(jnp.concatenate(ps, axis=0), vv)
            heads += [o_all[g * WINDOW:(g + 1) * WINDOW] / denoms[g] for g in range(Q_GROUP)]
        blocks.append(jnp.concatenate(heads, axis=1))
    attn = jnp.concatenate(blocks, axis=0)

    gvn = _layer_norm(gv, lng_ref[...], lnb_ref[...]).astype(BF16)
    rr = lax.broadcasted_iota(jnp.int32, (CHUNK, CHUNK), 0)
    cc = lax.broadcasted_iota(jnp.int32, (CHUNK, CHUNK), 1)
    gw = gm_w // GM_GROUPS
    cols = []
    for g in range(GM_GROUPS):
        wsg = jnp.where(rr >= cc, ws_ref[g], 0.0).astype(BF16)
        bias = bst_ref[:, g:g + 1]
        rows = [_dot(wsg, gvn[c * CHUNK:(c + 1) * CHUNK, g * gw:(g + 1) * gw]) + bias
                for c in range(t // CHUNK)]
        cols.append(jnp.concatenate(rows, axis=0))
    gm = gu * jnp.concatenate(cols, axis=1)

    z = sc * sh
    z1, z2 = _shift_rows_roll(ztail[...], z)
    ztail[...] = z[t - SUBLANES:]
    mc_ref[0] = z[t - SUBLANES:]
    scv = sb * (mcw_ref[0:1, :] * z2 + mcw_ref[1:2, :] * z1 + mcw_ref[2:3, :] * z)

    y_ref[0] = _merge_and_norm(x, xb, (attn, gm, scv), (w_gate0_ref, w_gate1_ref, w_gate2_ref), b_gate_ref,
                               (p_attn_ref, p_gmlp_ref, p_conv_ref), w_o_ref, ln1g_ref, ln1b_ref, alpha)


def _mixer_sample_kernel(sinks_ref, x_ref, ckt_ref, cvt_ref, mcs_ref, wkvt_ref, w_in_ref,
                         w_gate0_ref, w_gate1_ref, w_gate2_ref, b_gate_ref,
                         lng_ref, lnb_ref, ws_ref, bs8_ref, mcw_ref, p_attn_ref, p_gmlp_ref,
                         p_conv_ref, w_o_ref, ln1g_ref, ln1b_ref,
                         y_ref, knewt_ref, vnewt_ref, mc_ref, gvn_ref, wsbd, *, alpha, n_heads, layer):
    bt, t, d = x_ref.shape
    kv_w = ckt_ref.shape[1] * ckt_ref.shape[2]
    q_w = n_heads * HEAD_DIM
    gm_w = lng_ref.shape[1]
    sc_w = mcw_ref.shape[1]
    rows = bt * t

    x = x_ref[...].reshape(rows, d)
    xb = x.astype(BF16)
    off = [0]

    def proj(width):
        lo = off[0]
        off[0] = lo + width
        return _dot(xb, w_in_ref[:, lo:lo + width])

    q = (proj(q_w) * (HEAD_DIM ** -0.5 * LOG2_E)).reshape(bt, t, q_w)
    kv = proj(2 * kv_w)
    k = kv[:, :kv_w].reshape(bt, t, kv_w)
    v = kv[:, kv_w:].reshape(bt, t, kv_w)
    gu = proj(gm_w)
    gv = proj(gm_w)
    sb = proj(sc_w)
    sc = proj(sc_w)
    sh = proj(sc_w)

    kv_t = lax.dot_general(wkvt_ref[...], xb, (((1,), (1,)), ((), ())), preferred_element_type=F32)
    qrows = Q_GROUP * t

    def bias_for(n_keys, first_pos, kh):
        ri = lax.broadcasted_iota(jnp.int32, (qrows, n_keys), 0)
        ci = lax.broadcasted_iota(jnp.int32, (qrows, n_keys), 1)
        dist = ri % t + WINDOW - (ci + first_pos)
        valid = (dist >= 0) & (dist <= WINDOW)
        gi = ri // t
        slope = jnp.zeros((qrows, n_keys), F32)
        for g in range(Q_GROUP):
            h = kh * Q_GROUP + g
            slope = jnp.where(gi == g, 2.0 ** (-8.0 * (h + 1) / n_heads) * LOG2_E, slope)
        return jnp.where(valid, -slope * dist.astype(F32), -jnp.inf)

    head_outs = []
    for kh in range(n_heads // Q_GROUP):
        lo, hi = kh * HEAD_DIM, (kh + 1) * HEAD_DIM
        kt = ckt_ref[:, kh]
        vt = cvt_ref[:, kh]
        k_new = k[:, :, lo:hi].astype(BF16)
        v_new = v[:, :, lo:hi].astype(BF16)
        qg = jnp.concatenate(
            [q[:, :, (kh * Q_GROUP + g) * HEAD_DIM:(kh * Q_GROUP + g + 1) * HEAD_DIM] for g in range(Q_GROUP)],
            axis=1).astype(BF16)
        s_old = jnp.einsum("bqd,bdk->bqk", qg, kt.astype(BF16), preferred_element_type=F32)
        s_new = jnp.einsum("bqd,bkd->bqk", qg, k_new, preferred_element_type=F32)
        s_old = s_old + bias_for(WINDOW, 0, kh)[None]
        s_new = s_new + bias_for(t, WINDOW, kh)[None]
        gi = lax.broadcasted_iota(jnp.int32, (qrows, 1), 0) // t
        sink = jnp.zeros((qrows, 1), F32)
        for g in range(Q_GROUP):
            sink = jnp.where(gi == g, sinks_ref[layer, kh * Q_GROUP + g] * LOG2_E, sink)
        sink = sink[None]
        m = jnp.maximum(jnp.maximum(jnp.max(s_old, axis=-1, keepdims=True),
                                    jnp.max(s_new, axis=-1, keepdims=True)), sink)
        p_old = jnp.exp2(s_old - m)
        p_new = jnp.exp2(s_new - m)
        denom = (jnp.sum(p_old, axis=-1, keepdims=True) + jnp.sum(p_new, axis=-1, keepdims=True)
                 + jnp.exp2(sink - m))
        o = (jnp.einsum("bqk,bdk->bqd", p_old.astype(BF16), vt.astype(BF16), preferred_element_type=F32)
             + jnp.einsum("bqk,bkd->bqd", p_new.astype(BF16), v_new, preferred_element_type=F32)) / denom
        head_outs += [o[:, g * t:(g + 1) * t, :] for g in range(Q_GROUP)]
        for b in range(bt):
            knewt_ref[b, kh] = jnp.concatenate([kt[b][:, t:], kv_t[lo:hi, b * t:(b + 1) * t]], axis=1)
            vnewt_ref[b, kh] = jnp.concatenate(
                [vt[b][:, t:], kv_t[kv_w + lo:kv_w + hi, b * t:(b + 1) * t]], axis=1)
    attn = jnp.concatenate(head_outs, axis=2).reshape(rows, q_w)

    gvn = _layer_norm(gv, lng_ref[...], lnb_ref[...])
    gvn_ref[...] = gvn.reshape(bt, t, gm_w)
    gvb = gvn.astype(BF16)
    gw = gm_w // GM_GROUPS

    @pl.when(pl.program_id(0) == 0)
    def _():
        for g in range(GM_GROUPS):
            wsbd[g] = _block_diag_causal(ws_ref[g], t, rows)

    sv = jnp.concatenate([_dot(wsbd[g], gvb[:, g * gw:(g + 1) * gw]) for g in range(GM_GROUPS)], axis=1)
    sv = sv.reshape(bt, t, gm_w) + bs8_ref[...][None]
    gm = gu * sv.reshape(rows, gm_w)

    z = (sc * sh).reshape(bt, t, sc_w)
    z1, z2 = _shift_rows_3d(mcs_ref[...], z)
    mc_ref[...] = z[:, t - (CONV_W - 1):, :]
    cz = mcw_ref[0:1, :][None] * z2 + mcw_ref[1:2, :][None] * z1 + mcw_ref[2:3, :][None] * z
    scv = sb * cz.reshape(rows, sc_w)

    y = _merge_and_norm(x, xb, (attn, gm, scv), (w_gate0_ref, w_gate1_ref, w_gate2_ref), b_gate_ref,
                        (p_attn_ref, p_gmlp_ref, p_conv_ref), w_o_ref, ln1g_ref, ln1b_ref, alpha)
    y_ref[...] = y.reshape(bt, t, d)


def _ffn_hidden(x, up_refs, cw_ref, cb_ref, shift, store_up):
    d_ff = up_refs[0].shape[1]
    xb = x.astype(BF16)
    hs = []
    for j in range(d_ff // FF_CHUNK):
        parts = []
        for up_ref, base in zip(up_refs, (0, d_ff)):
            lo = base + j * FF_CHUNK
            up = _dot(xb, up_ref[:, j * FF_CHUNK:(j + 1) * FF_CHUNK])
            u1, u2 = shift(up, lo)
            store_up(up, lo)
            parts.append(cw_ref[0:1, lo:lo + FF_CHUNK] * u2 + cw_ref[1:2, lo:lo + FF_CHUNK] * u1
                         + cw_ref[2:3, lo:lo + FF_CHUNK] * up + cb_ref[:, lo:lo + FF_CHUNK])
        a, g = parts
        hs.append((g * _sigmoid(g) * a).astype(BF16))
    return jnp.concatenate(hs, axis=1)


def _ffn_out(x, h, w_down_ref, g_ref, b_ref, alpha):
    return _project_residual_norm(h, w_down_ref, x, g_ref, b_ref, alpha)


def _ffn_prompt_kernel(x_ref, up_a_ref, up_g_ref, cw_ref, cb_ref, w_down_ref, g_ref, b_ref,
                       y_ref, fc_ref, ubuf, *, alpha):
    t = x_ref.shape[1]

    @pl.when(pl.program_id(1) == 0)
    def _():
        ubuf[...] = jnp.zeros_like(ubuf)

    def shift(up, lo):
        return _shift_rows_roll(ubuf[:, lo:lo + FF_CHUNK], up)

    def store_up(up, lo):
        ubuf[:, lo:lo + FF_CHUNK] = up[t - SUBLANES:]
        fc_ref[0, :, lo:lo + FF_CHUNK] = up[t - SUBLANES:]

    x = x_ref[0]
    h = _ffn_hidden(x, (up_a_ref, up_g_ref), cw_ref, cb_ref, shift, store_up)
    y_ref[0] = _ffn_out(x, h, w_down_ref, g_ref, b_ref, alpha)


def _ffn_sample_kernel(x_ref, past_ref, up_a_ref, up_g_ref, cw_ref, cb_ref, w_down_ref, g_ref, b_ref,
                       y_ref, fc_ref, *, alpha):
    bt, t, d = x_ref.shape

    def shift(up, lo):
        u1, u2 = _shift_rows_3d(past_ref[:, :, lo:lo + FF_CHUNK], up.reshape(bt, t, FF_CHUNK))
        return u1.reshape(bt * t, FF_CHUNK), u2.reshape(bt * t, FF_CHUNK)

    def store_up(up, lo):
        fc_ref[:, :, lo:lo + FF_CHUNK] = up.reshape(bt, t, FF_CHUNK)[:, t - (CONV_W - 1):, :]

    x = x_ref[...].reshape(bt * t, d)
    h = _ffn_hidden(x, (up_a_ref, up_g_ref), cw_ref, cb_ref, shift, store_up)
    y = _ffn_out(x, h, w_down_ref, g_ref, b_ref, alpha)
    y_ref[...] = y.reshape(bt, t, d)


class _ColumnSlab:
    def __init__(self, array, index, count):
        self.array, self.index, self.count = array, index, count


def _operand(a):
    return a.array if isinstance(a, _ColumnSlab) else a


def _resident(a, layer):
    if isinstance(a, _ColumnSlab):
        rows, cols = a.array.shape[1:]
        idx = (layer, 0, a.index)
        return pl.BlockSpec((None, rows, cols // a.count), lambda *_: idx, pipeline_mode=pl.Buffered(1))
    idx = (layer,) + (0,) * (a.ndim - 1)
    return pl.BlockSpec((None,) + a.shape[1:], lambda *_: idx, pipeline_mode=pl.Buffered(1))


_MIXER_PROMPT_WEIGHTS = ("w_in", "w_gate0", "w_gate1", "w_gate2", "b_gate", "lng", "lnb", "ws",
                         "bst", "mcw", "p_attn", "p_gmlp", "p_conv", "w_o", "ln1g", "ln1b")
_MIXER_SAMPLE_WEIGHTS = ("w_in", "w_gate0", "w_gate1", "w_gate2", "b_gate", "lng", "lnb", "ws",
                         "bs8", "mcw", "p_attn", "p_gmlp", "p_conv", "w_o", "ln1g", "ln1b")
_FFN_WEIGHTS = ("w_up_a", "w_up_g", "fcw", "fcb", "w_down", "ln2g", "ln2b")


def _params():
    return pltpu.CompilerParams(dimension_semantics=("arbitrary", "arbitrary"),
                                vmem_limit_bytes=VMEM_LIMIT_BYTES)


def _mixer_prompt(x, w, layer, alpha, n_heads):
    b, s, d = x.shape
    t = PROMPT_TILE
    kv_w = (n_heads // Q_GROUP) * HEAD_DIM
    sc_w = w["mcw"].shape[-1]
    weights = [w[k] for k in _MIXER_PROMPT_WEIGHTS]
    tile = pl.BlockSpec((1, t, d), lambda i, j: (i, j, 0))
    per_seq = lambda rows, width: pl.BlockSpec((1, rows, width), lambda i, j: (i, 0, 0))
    return pl.pallas_call(
        functools.partial(_mixer_prompt_kernel, alpha=alpha, n_heads=n_heads, layer=layer),
        grid=(b, s // t),
        in_specs=[pl.BlockSpec(memory_space=pltpu.SMEM), tile] + [_resident(a, layer) for a in weights],
        out_specs=[tile, per_seq(WINDOW, kv_w), per_seq(WINDOW, kv_w), per_seq(SUBLANES, sc_w)],
        out_shape=[jax.ShapeDtypeStruct((b, s, d), F32),
                   jax.ShapeDtypeStruct((b, WINDOW, kv_w), F32),
                   jax.ShapeDtypeStruct((b, WINDOW, kv_w), F32),
                   jax.ShapeDtypeStruct((b, SUBLANES, sc_w), F32)],
        scratch_shapes=[pltpu.VMEM((WINDOW, kv_w), F32), pltpu.VMEM((WINDOW, kv_w), F32),
                        pltpu.VMEM((SUBLANES, sc_w), F32),
                        pltpu.VMEM((n_heads, WINDOW, 2 * WINDOW), F32)],
        compiler_params=_params(),
        name="mixer_prompt",
    )(w["sinks"], x, *map(_operand, weights))


def _mixer_sample(x, ckt, cvt, mcs, w, layer, alpha, n_heads):
    b, t, d = x.shape
    bt = SAMPLE_SEQS
    kv_heads = ckt.shape[2]
    gm_w = w["lng"].shape[-1]
    sc_w = w["mcw"].shape[-1]
    weights = [w["wkvt"]] + [w[k] for k in _MIXER_SAMPLE_WEIGHTS]
    seqs = lambda rows, width: pl.BlockSpec((bt, rows, width), lambda i, j: (i, 0, 0))
    state = lambda rows, width: pl.BlockSpec((None, bt, rows, width), lambda i, j: (layer, i, 0, 0))
    window_in = pl.BlockSpec((None, bt, kv_heads, HEAD_DIM, WINDOW), lambda i, j: (layer, i, 0, 0, 0))
    window_out = pl.BlockSpec((bt, kv_heads, HEAD_DIM, WINDOW), lambda i, j: (i, 0, 0, 0))
    return pl.pallas_call(
        functools.partial(_mixer_sample_kernel, alpha=alpha, n_heads=n_heads, layer=layer),
        grid=(b // bt, 1),
        in_specs=[pl.BlockSpec(memory_space=pltpu.SMEM), seqs(t, d), window_in, window_in,
                  state(CONV_W - 1, sc_w)] + [_resident(a, layer) for a in weights],
        out_specs=[seqs(t, d), window_out, window_out, seqs(CONV_W - 1, sc_w), seqs(t, gm_w)],
        out_shape=[jax.ShapeDtypeStruct((b, t, d), F32),
                   jax.ShapeDtypeStruct((b, kv_heads, HEAD_DIM, WINDOW), F32),
                   jax.ShapeDtypeStruct((b, kv_heads, HEAD_DIM, WINDOW), F32),
                   jax.ShapeDtypeStruct((b, CONV_W - 1, sc_w), F32),
                   jax.ShapeDtypeStruct((b, t, gm_w), F32)],
        scratch_shapes=[pltpu.VMEM((GM_GROUPS, bt * t, bt * t), BF16)],
        compiler_params=_params(),
        name="mixer_sample",
    )(w["sinks"], x, ckt, cvt, mcs, *map(_operand, weights))


def _ffn_prompt(x, w, layer, alpha):
    b, s, d = x.shape
    t = FFN_TILE
    d_ff = w["w_down"].shape[1]
    weights = [w[k] for k in _FFN_WEIGHTS]
    tile = pl.BlockSpec((1, t, d), lambda i, j: (i, j, 0))
    return pl.pallas_call(
        functools.partial(_ffn_prompt_kernel, alpha=alpha),
        grid=(b, s // t),
        in_specs=[tile] + [_resident(a, layer) for a in weights],
        out_specs=[tile, pl.BlockSpec((1, SUBLANES, 2 * d_ff), lambda i, j: (i, 0, 0))],
        out_shape=[jax.ShapeDtypeStruct((b, s, d), F32),
                   jax.ShapeDtypeStruct((b, SUBLANES, 2 * d_ff), F32)],
        scratch_shapes=[pltpu.VMEM((SUBLANES, 2 * d_ff), F32)],
        compiler_params=_params(),
        name="ffn_prompt",
    )(x, *map(_operand, weights))


def _ffn_sample(x, past, w, layer, alpha):
    b, t, d = x.shape
    bt = SAMPLE_SEQS
    d_ff = w["w_down"].shape[1]
    weights = [w[k] for k in _FFN_WEIGHTS]
    seqs = lambda rows, width: pl.BlockSpec((bt, rows, width), lambda i, j: (i, 0, 0))
    return pl.pallas_call(
        functools.partial(_ffn_sample_kernel, alpha=alpha),
        grid=(b // bt, 1),
        in_specs=[seqs(t, d), pl.BlockSpec((None, bt, CONV_W - 1, 2 * d_ff), lambda i, j: (layer, i, 0, 0))]
        + [_resident(a, layer) for a in weights],
        out_specs=[seqs(t, d), seqs(CONV_W - 1, 2 * d_ff)],
        out_shape=[jax.ShapeDtypeStruct((b, t, d), F32),
                   jax.ShapeDtypeStruct((b, CONV_W - 1, 2 * d_ff), F32)],
        compiler_params=_params(),
        name="ffn_sample",
    )(x, past, *map(_operand, weights))


def kernel(x_prompt, x_sample, cache_k_win, cache_v_win, state_mixconv, state_ffnconv, w_in, w_gate, b_gate, gmlp_ln_g, gmlp_ln_b, gmlp_ws, gmlp_bs, mixconv_w, attn_sinks, p_attn, p_gmlp, p_conv, w_o, ln1_g, ln1_b, w_up, ffn_conv_w, ffn_conv_b, w_down, ln2_g, ln2_b):
    depth = w_in.shape[0]
    d = x_prompt.shape[-1]
    n_heads = d // 128
    kv_heads = n_heads // Q_GROUP
    alpha = (2.0 * depth) ** 0.25
    dec_b, dec_t = x_sample.shape[:2]
    gm_w = gmlp_ln_g.shape[1]
    gw = gm_w // GM_GROUPS
    q_w, kv_w = n_heads * HEAD_DIM, kv_heads * HEAD_DIM
    assert x_prompt.shape[1] % PROMPT_TILE == 0 and x_prompt.shape[1] % FFN_TILE == 0
    assert dec_t == SUBLANES and dec_b % SAMPLE_SEQS == 0 and CONV_W - 1 <= SUBLANES
    assert n_heads % Q_GROUP == 0 and w_down.shape[1] % FF_CHUNK == 0 and gm_w % GM_GROUPS == 0

    bs8 = jnp.repeat(jnp.swapaxes(gmlp_bs[:, :, :dec_t], 1, 2), gw, axis=2)

    row = lambda a: a[:, None, :]
    w_gate_b, w_up_b = w_gate.astype(BF16), w_up.astype(BF16)
    w = dict(
        sinks=attn_sinks,
        w_in=w_in.astype(BF16),
        wkvt=jnp.swapaxes(w_in[:, :, q_w:q_w + 2 * kv_w], 1, 2).astype(BF16),
        w_gate0=_ColumnSlab(w_gate_b, 0, N_BRANCH), w_gate1=_ColumnSlab(w_gate_b, 1, N_BRANCH),
        w_gate2=_ColumnSlab(w_gate_b, 2, N_BRANCH), b_gate=row(b_gate),
        lng=row(gmlp_ln_g), lnb=row(gmlp_ln_b), ws=gmlp_ws, bst=jnp.swapaxes(gmlp_bs, 1, 2),
        bs8=bs8, mcw=mixconv_w,
        p_attn=p_attn.astype(BF16), p_gmlp=p_gmlp.astype(BF16), p_conv=p_conv.astype(BF16),
        w_o=w_o.astype(BF16), ln1g=row(ln1_g), ln1b=row(ln1_b),
        w_up_a=_ColumnSlab(w_up_b, 0, 2), w_up_g=_ColumnSlab(w_up_b, 1, 2), fcw=ffn_conv_w, fcb=row(ffn_conv_b),
        w_down=w_down.astype(BF16), ln2g=row(ln2_g), ln2b=row(ln2_b))

    xp, xs = x_prompt, x_sample
    ck_all = jnp.transpose(cache_k_win, (0, 1, 3, 4, 2))
    cv_all = jnp.transpose(cache_v_win, (0, 1, 3, 4, 2))
    outs = {k: [] for k in ("kp", "vp", "mcp", "fcp", "ks", "vs", "mcs", "fcs", "gvs")}
    for l in range(depth):
        xp, kwin, vwin, mc = _mixer_prompt(xp, w, l, alpha, n_heads)
        xp, fc = _ffn_prompt(xp, w, l, alpha)
        xs, knew, vnew, mcn, gvn = _mixer_sample(xs, ck_all, cv_all, state_mixconv, w, l, alpha, n_heads)
        xs, fcn = _ffn_sample(xs, state_ffnconv, w, l, alpha)
        for key, val in zip(outs, (kwin, vwin, mc, fc, knew, vnew, mcn, fcn, gvn)):
            outs[key].append(val)
    st = {k: jnp.stack(v) for k, v in outs.items()}
    heads = lambda a: a.reshape(a.shape[:3] + (kv_heads, HEAD_DIM))
    last = lambda a: a[:, :, a.shape[2] - (CONV_W - 1):]
    from_t = lambda a: jnp.transpose(a, (0, 1, 4, 2, 3))
    return (xp, xs, heads(st["kp"]), heads(st["vp"]), last(st["mcp"]), last(st["fcp"]),
            from_t(st["ks"]), from_t(st["vs"]), st["mcs"], st["fcs"], st["gvs"])
```
